```python
import jax
import jax.numpy as jnp
from jax import lax
import numpy as np

D_MODEL = 2048
BATCH = 4
SEQ = 4096
DEPTH = 2

HEAD_DIM = 128
ROPE_THETA = 10000.0
RMS_EPS = 1e-5
Q_BLOCK = 64

A_HEADS = 8
KV_RANK = 512
IDX_HEADS = 16
IDX_DIM = 64
TOPK_MAX = 256

B_PATTERNS = ((128, 1), (512, 4), (2048, 16))
B_GROUPS = 3
B_HEADS_PER_GROUP = 4

N_BRANCH = 2
N_MOD = 6

N_EXPERTS = 32
TOP_K = 4
D_FF = D_MODEL
SWIGLU_LIMIT = 7.0
SWIGLU_ALPHA = 1.702

A_Q_WIDTH = A_HEADS * HEAD_DIM
IDX_Q_WIDTH = IDX_HEADS * IDX_DIM
B_QKV_WIDTH = 3 * B_GROUPS * B_HEADS_PER_GROUP * HEAD_DIM
GATE_WIDTH = N_BRANCH * D_MODEL
IN_SIZES = (A_Q_WIDTH, KV_RANK, IDX_Q_WIDTH, IDX_DIM, IDX_HEADS, B_QKV_WIDTH, GATE_WIDTH)
D_IN = A_Q_WIDTH + KV_RANK + IDX_Q_WIDTH + IDX_DIM + IDX_HEADS + B_QKV_WIDTH + GATE_WIDTH
B_OUT_WIDTH = B_HEADS_PER_GROUP * HEAD_DIM

kernel_name = 'hybrid_dsa_dilated_moe_block'


def rmsnorm(x, g):
    xf = x.astype(jnp.float32)
    y = xf * lax.rsqrt(jnp.mean(xf * xf, axis=-1, keepdims=True) + RMS_EPS)
    return (y * g.astype(jnp.float32)).astype(x.dtype)


def modulate(h, shift, scale):
    return h * (1 + scale) + shift


def rope_tables(positions, dim):
    inv_freq = 1.0 / (ROPE_THETA ** (jnp.arange(0, dim, 2, dtype=jnp.float32) / dim))
    ang = positions.astype(jnp.float32)[..., None] * inv_freq
    return jnp.cos(ang), jnp.sin(ang)


def apply_rope(x, cos, sin):
    shape = cos.shape[:2] + (1,) * (x.ndim - 3) + cos.shape[-1:]
    cos = cos.reshape(shape)
    sin = sin.reshape(shape)
    x1, x2 = jnp.split(x.astype(jnp.float32), 2, axis=-1)
    return jnp.concatenate([x1 * cos - x2 * sin, x2 * cos + x1 * sin], axis=-1).astype(x.dtype)


def dsa_attention(q, k, v, q_idx, k_idx, w_idx):
    bsz, seq_len = q.shape[0], q.shape[1]
    n_sel = min(TOPK_MAX, seq_len // 4)
    idx_scale = (IDX_DIM ** -0.5) * (IDX_HEADS ** -0.5)
    attn_scale = HEAD_DIM ** -0.5
    key_pos = jnp.arange(seq_len)
    k_idx_f = k_idx.astype(jnp.float32)

    def block(i):
        t0 = i * Q_BLOCK
        t = t0 + jnp.arange(Q_BLOCK)
        qb = lax.dynamic_slice_in_dim(q, t0, Q_BLOCK, axis=1)
        qib = lax.dynamic_slice_in_dim(q_idx, t0, Q_BLOCK, axis=1).astype(jnp.float32)
        wib = lax.dynamic_slice_in_dim(w_idx, t0, Q_BLOCK, axis=1).astype(jnp.float32)
        dots = jnp.einsum('bqhd,bsd->bhqs', qib, k_idx_f)
        score = jnp.einsum('bhqs,bqh->bqs', jax.nn.relu(dots), wib) * idx_scale
        causal = key_pos[None, :] <= t[:, None]
        score = jnp.where(causal[None], score, -jnp.inf)
        _, sel = lax.top_k(score, n_sel)
        valid = sel <= t[None, :, None]
        kg = jax.vmap(lambda kk, ii: kk[ii])(k, sel)
        vg = jax.vmap(lambda vv, ii: vv[ii])(v, sel)
        logits = jnp.einsum('bqhd,bqkhd->bhqk', qb, kg, preferred_element_type=jnp.float32) * attn_scale
        logits = jnp.where(valid[:, None], logits, -jnp.inf)
        p = jax.nn.softmax(logits, axis=-1)
        o = jnp.einsum('bhqk,bqkhd->bqhd', p.astype(v.dtype), vg, preferred_element_type=jnp.float32)
        return o.astype(q.dtype)

    out = lax.map(block, jnp.arange(seq_len // Q_BLOCK))
    return jnp.moveaxis(out, 0, 1).reshape(bsz, seq_len, A_Q_WIDTH)


def dilated_attention(q, k, v):
    bsz, seq_len = q.shape[0], q.shape[1]
    scale = HEAD_DIM ** -0.5
    k_groups = [k[:, :, g] for g in range(B_GROUPS)]
    v_groups = [v[:, :, g] for g in range(B_GROUPS)]

    def block(i):
        t0 = i * Q_BLOCK
        t = t0 + jnp.arange(Q_BLOCK)
        qblk = lax.dynamic_slice_in_dim(q, t0, Q_BLOCK, axis=1)
        outs, lses = [], []
        for g, (window, dilation) in enumerate(B_PATTERNS):
            dist = dilation * jnp.arange(window // dilation + 1)
            pos = t[:, None] - dist[None, :]
            valid = pos >= 0
            pos = jnp.maximum(pos, 0)
            kg = jnp.take(k_groups[g], pos, axis=1)
            vg = jnp.take(v_groups[g], pos, axis=1)
            logits = jnp.einsum('bqhd,bqjhd->bhqj', qblk[:, :, g], kg, preferred_element_type=jnp.float32) * scale
            logits = jnp.where(valid[None, None], logits, -jnp.inf)
            lse = jax.nn.logsumexp(logits, axis=-1)
            p = jnp.exp(logits - lse[..., None])
            o = jnp.einsum('bhqj,bqjhd->bqhd', p.astype(v.dtype), vg, preferred_element_type=jnp.float32)
            outs.append(o)
            lses.append(lse)
        wg = jax.nn.softmax(jnp.stack(lses, axis=-1), axis=-1)
        wg = jnp.transpose(wg, (0, 2, 1, 3))[:, :, :, None, :]
        o = jnp.sum(jnp.stack(outs, axis=-1) * wg, axis=-1)
        return o.astype(q.dtype)

    out = lax.map(block, jnp.arange(seq_len // Q_BLOCK))
    return jnp.moveaxis(out, 0, 1).reshape(bsz, seq_len, B_OUT_WIDTH)


def hybrid_mixer(h, cos, sin, cos_i, sin_i, w_in, g_kv, w_kv_up, w_proj_a, w_proj_b, w_out):
    bsz, seq_len, _ = h.shape
    proj = h @ w_in
    offs = [int(o) for o in np.cumsum(IN_SIZES)[:-1]]
    qa, ckv, qi, ki, wi, qkv_b, gate_logits = jnp.split(proj, offs, axis=-1)
    qa = apply_rope(qa.reshape(bsz, seq_len, A_HEADS, HEAD_DIM), cos, sin)
    kv = (rmsnorm(ckv, g_kv) @ w_kv_up).reshape(bsz, seq_len, A_HEADS, 2, HEAD_DIM)
    ka = apply_rope(kv[:, :, :, 0], cos, sin)
    va = kv[:, :, :, 1]
    qi = apply_rope(qi.reshape(bsz, seq_len, IDX_HEADS, IDX_DIM), cos_i, sin_i)
    ki = apply_rope(ki, cos_i, sin_i)
    o_a = dsa_attention(qa, ka, va, qi, ki, wi)
    qkv_b = qkv_b.reshape(bsz, seq_len, 3, B_GROUPS, B_HEADS_PER_GROUP, HEAD_DIM)
    qb = apply_rope(qkv_b[:, :, 0], cos, sin)
    kb = apply_rope(qkv_b[:, :, 1], cos, sin)
    vb = qkv_b[:, :, 2]
    o_b = dilated_attention(qb, kb, vb)
    gates = jax.nn.sigmoid(gate_logits).reshape(bsz, seq_len, N_BRANCH, D_MODEL)
    merged = gates[:, :, 0] * (o_a @ w_proj_a) + gates[:, :, 1] * (o_b @ w_proj_b)
    return merged @ w_out


def moe_ffn(h, w_router, b_router, w1, b1, w2, b2):
    logits = (h @ w_router + b_router).astype(jnp.float32)
    top_val, top_idx = lax.top_k(logits, TOP_K)
    top_w = jax.nn.softmax(top_val, axis=-1)
    combine = jnp.sum(jax.nn.one_hot(top_idx, N_EXPERTS, dtype=jnp.float32) * top_w[..., None], axis=-2)
    out = jnp.zeros(h.shape, jnp.float32)
    for e in range(N_EXPERTS):
        a = h @ w1[e] + b1[e]
        glu = jnp.minimum(a[..., ::2], SWIGLU_LIMIT)
        lin = jnp.clip(a[..., 1::2], -SWIGLU_LIMIT, SWIGLU_LIMIT)
        act = glu * jax.nn.sigmoid(SWIGLU_ALPHA * glu) * (lin + 1)
        y = act @ w2[e] + b2[e]
        out = out + combine[..., e:e + 1] * y.astype(jnp.float32)
    return out.astype(h.dtype)


def setup_inputs(seed: int = 0) -> dict:
    key = jax.random.key(seed)
    ks = jax.random.split(key, 24)
    f32 = jnp.float32

    def nrm(k, shape, scale):
        return jax.random.normal(k, shape, f32) * scale

    x = jax.random.normal(ks[0], (BATCH, SEQ, D_MODEL), f32)
    c = jax.random.normal(ks[1], (BATCH, D_MODEL), f32)
    positions = jnp.arange(SEQ, dtype=jnp.int32)[None, :] + jax.random.randint(ks[2], (BATCH, 1), 0, 1024, dtype=jnp.int32)
    return {
        'x': x,
        'c': c,
        'positions': positions,
        'w_mod': nrm(ks[3], (DEPTH, D_MODEL, N_MOD * D_MODEL), 0.5 * D_MODEL ** -0.5),
        'b_mod': nrm(ks[4], (DEPTH, N_MOD * D_MODEL), 0.01),
        'g_norm1': 1.0 + nrm(ks[5], (DEPTH, D_MODEL), 0.02),
        'g_norm2': 1.0 + nrm(ks[6], (DEPTH, D_MODEL), 0.02),
        'w_in': nrm(ks[7], (DEPTH, D_MODEL, D_IN), D_MODEL ** -0.5),
        'g_kv': 1.0 + nrm(ks[8], (DEPTH, KV_RANK), 0.02),
        'w_kv_up': nrm(ks[9], (DEPTH, KV_RANK, A_HEADS * 2 * HEAD_DIM), KV_RANK ** -0.5),
        'w_proj_a': nrm(ks[10], (DEPTH, A_Q_WIDTH, D_MODEL), A_Q_WIDTH ** -0.5),
        'w_proj_b': nrm(ks[11], (DEPTH, B_OUT_WIDTH, D_MODEL), B_OUT_WIDTH ** -0.5),
        'w_out': nrm(ks[12], (DEPTH, D_MODEL, D_MODEL), D_MODEL ** -0.5),
        'w_router': nrm(ks[13], (DEPTH, D_MODEL, N_EXPERTS), D_MODEL ** -0.5),
        'b_router': nrm(ks[14], (DEPTH, N_EXPERTS), 0.01),
        'w_moe1': nrm(ks[15], (DEPTH, N_EXPERTS, D_MODEL, 2 * D_FF), D_MODEL ** -0.5),
        'b_moe1': nrm(ks[16], (DEPTH, N_EXPERTS, 2 * D_FF), 0.01),
        'w_moe2': nrm(ks[17], (DEPTH, N_EXPERTS, D_FF, D_MODEL), D_FF ** -0.5),
        'b_moe2': nrm(ks[18], (DEPTH, N_EXPERTS, D_MODEL), 0.01),
        'g_final': 1.0 + nrm(ks[19], (D_MODEL,), 0.02),
    }


def reference(x, c, positions, w_mod, b_mod, g_norm1, g_norm2, w_in, g_kv, w_kv_up, w_proj_a, w_proj_b, w_out, w_router, b_router, w_moe1, b_moe1, w_moe2, b_moe2, g_final):
    cos, sin = rope_tables(positions, HEAD_DIM)
    cos_i, sin_i = rope_tables(positions, IDX_DIM)
    c_act = jax.nn.silu(c)
    for l in range(DEPTH):
        mod = (c_act @ w_mod[l] + b_mod[l])[:, None, :]
        sh1, sc1, gt1, sh2, sc2, gt2 = jnp.split(mod, N_MOD, axis=-1)
        h = modulate(rmsnorm(x, g_norm1[l]), sh1, sc1)
        x = x + gt1 * hybrid_mixer(h, cos, sin, cos_i, sin_i, w_in[l], g_kv[l], w_kv_up[l], w_proj_a[l], w_proj_b[l], w_out[l])
        h = modulate(rmsnorm(x, g_norm2[l]), sh2, sc2)
        x = x + gt2 * moe_ffn(h, w_router[l], b_router[l], w_moe1[l], b_moe1[l], w_moe2[l], b_moe2[l])
    return rmsnorm(x, g_final)
```

```python
import functools

import jax
import jax.numpy as jnp
from jax import lax
from jax.experimental import pallas as pl
from jax.experimental.pallas import tpu as pltpu

HEAD_DIM = 128
ROPE_THETA = 10000.0
RMS_EPS = 1e-5
A_HEADS = 8
KV_RANK = 512
IDX_HEADS = 16
IDX_DIM = 64
TOPK_MAX = 256
B_PATTERNS = ((128, 1), (512, 4), (2048, 16))
B_GROUPS = 3
B_HEADS_PER_GROUP = 4
N_MOD = 6
TOP_K = 4
SWIGLU_LIMIT = 7.0
SWIGLU_ALPHA = 1.702

LANES = 128
SLAB_ROWS = 8
NEG = -1e30
INT_MIN = -(2 ** 31)
VMEM_LIMIT = 48 * 1024 * 1024

A_Q_WIDTH = A_HEADS * HEAD_DIM
IDX_Q_WIDTH = IDX_HEADS * IDX_DIM
B_OUT_WIDTH = B_HEADS_PER_GROUP * HEAD_DIM
B_QKV_WIDTH = 3 * B_GROUPS * B_OUT_WIDTH

F32 = jnp.float32
BF16 = jnp.bfloat16
NT_DIMS = (((1,), (1,)), ((), ()))


def _tile(n, pref):
    return pref if n % pref == 0 else n


def _params(*sem):
    return pltpu.CompilerParams(dimension_semantics=sem, vmem_limit_bytes=VMEM_LIMIT)


def _dot(a, b):
    return jnp.dot(a, b, preferred_element_type=F32)


def _dot_nt(a, b):
    return lax.dot_general(a, b, NT_DIMS, preferred_element_type=F32)


def _mod_kernel(c_ref, w_ref, b_ref, o_ref):
    c = c_ref[...]
    c_act = c * jax.nn.sigmoid(c)
    o_ref[...] = _dot(c_act, w_ref[...]) + b_ref[...]


def _modulation(c, w_mod, b_mod):
    depth, d, n = w_mod.shape
    bsz = c.shape[0]
    tn = _tile(n, 1024)
    return pl.pallas_call(
        _mod_kernel,
        grid=(depth, n // tn),
        in_specs=[
            pl.BlockSpec((bsz, d), lambda l, j: (0, 0)),
            pl.BlockSpec((None, d, tn), lambda l, j: (l, 0, j)),
            pl.BlockSpec((None, 1, tn), lambda l, j: (l, 0, j)),
        ],
        out_specs=pl.BlockSpec((None, bsz, tn), lambda l, j: (l, 0, j)),
        out_shape=jax.ShapeDtypeStruct((depth, bsz, n), F32),
        compiler_params=_params("parallel", "parallel"),
        name="adaln_mod",
    )(c, w_mod, b_mod.reshape(depth, 1, n))


def _store_slabs(ref, y):
    c = ref.shape[2]
    for s in range(SLAB_ROWS):
        ref[:, s, :] = y[:, s * c:(s + 1) * c].astype(ref.dtype)


def _load_slabs(ref, dtype):
    return jnp.concatenate([ref[:, s, :].astype(dtype) for s in range(SLAB_ROWS)], axis=1)


def _norm_kernel(x_ref, g_ref, *rest, modulate, slab):
    x = x_ref[...]
    y = x * lax.rsqrt(jnp.mean(x * x, axis=-1, keepdims=True) + RMS_EPS) * g_ref[...]
    if modulate:
        sh_ref, sc_ref, o_ref = rest
        y = y * (1.0 + sc_ref[...]) + sh_ref[...]
    else:
        (o_ref,) = rest
    if slab:
        _store_slabs(o_ref, y)
    else:
        o_ref[...] = y.astype(o_ref.dtype)


def _norm(x2, g, seq, shift=None, scale=None, out_dtype=BF16, slab=False):
    t, d = x2.shape
    tm = _tile(seq, 512)
    per_b = seq // tm
    in_specs = [pl.BlockSpec((tm, d), lambda i: (i, 0)), pl.BlockSpec((1, d), lambda i: (0, 0))]
    args = [x2, g.reshape(1, d)]
    if shift is not None:
        vec = pl.BlockSpec((None, 1, d), lambda i: (i // per_b, 0, 0))
        in_specs += [vec, vec]
        args += [shift, scale]
    if slab:
        c = d // SLAB_ROWS
        out_spec = pl.BlockSpec((tm, SLAB_ROWS, c), lambda i: (i, 0, 0))
        out_shape = jax.ShapeDtypeStruct((t, SLAB_ROWS, c), out_dtype)
    else:
        out_spec = pl.BlockSpec((tm, d), lambda i: (i, 0))
        out_shape = jax.ShapeDtypeStruct((t, d), out_dtype)
    return pl.pallas_call(
        functools.partial(_norm_kernel, modulate=shift is not None, slab=slab),
        grid=(t // tm,),
        in_specs=in_specs,
        out_specs=out_spec,
        out_shape=out_shape,
        compiler_params=_params("parallel"),
        name="rmsnorm_mod",
    )(*args)


def _rope_cols(x, cos, sin, half):
    outs = []
    for c in range(x.shape[1] // LANES):
        xc = x[:, c * LANES:(c + 1) * LANES]
        if half == LANES // 2:
            partner = pltpu.roll(xc, LANES // 2, 1)
        else:
            lane = lax.broadcasted_iota(jnp.int32, xc.shape, 1)
            low = (lane & (2 * half - 1)) < half
            partner = jnp.where(low, pltpu.roll(xc, LANES - half, 1), pltpu.roll(xc, half, 1))
        outs.append(xc * cos + partner * sin)
    return outs[0] if len(outs) == 1 else jnp.concatenate(outs, axis=1)


def _mm_kernel(h_ref, w_ref, *rest, epi, rope_cols):
    acc = _dot(h_ref[...], w_ref[...])
    if epi in ("rope64", "rope32"):
        cos_ref, sin_ref, o_ref = rest
        half = 64 if epi == "rope64" else 32
        if rope_cols is None:
            acc = _rope_cols(acc, cos_ref[...], sin_ref[...], half)
        else:
            j = pl.program_id(1)
            roped = _rope_cols(acc, cos_ref[...], sin_ref[...], half)
            acc = jnp.where(j < rope_cols, roped, acc)
    elif epi == "sigmoid":
        (o_ref,) = rest
        acc = jax.nn.sigmoid(acc)
    elif epi == "resid":
        x_ref, gt_ref, o_ref = rest
        acc = x_ref[...] + gt_ref[...] * acc
    else:
        (o_ref,) = rest
    o_ref[...] = acc.astype(o_ref.dtype)


def _mm(h, w, *, epi="none", tables=None, resid=None, seq=None, out_dtype=BF16, rope_cols=None,
        tm_pref=1024, tn_pref=512):
    t, k = h.shape
    n = w.shape[1]
    tm = _tile(t if seq is None else seq, tm_pref)
    tn = _tile(n, tn_pref)
    in_specs = [pl.BlockSpec((tm, k), lambda i, j: (i, 0)), pl.BlockSpec((k, tn), lambda i, j: (0, j))]
    args = [h, w]
    if tables is not None:
        spec = pl.BlockSpec((tm, LANES), lambda i, j: (i, 0))
        in_specs += [spec, spec]
        args += list(tables)
    if resid is not None:
        x2, gt = resid
        per_b = seq // tm
        in_specs += [pl.BlockSpec((tm, tn), lambda i, j: (i, j)),
                     pl.BlockSpec((None, 1, tn), lambda i, j: (i // per_b, 0, j))]
        args += [x2, gt]
    return pl.pallas_call(
        functools.partial(_mm_kernel, epi=epi, rope_cols=rope_cols),
        grid=(t // tm, n // tn),
        in_specs=in_specs,
        out_specs=pl.BlockSpec((tm, tn), lambda i, j: (i, j)),
        out_shape=jax.ShapeDtypeStruct((t, n), out_dtype),
        compiler_params=_params("parallel", "parallel"),
        name="proj_" + epi,
    )(*args)


def _kv_kernel(h_ref, wc_ref, g_ref, wk_ref, wv_ref, cos_ref, sin_ref, k_ref, v_ref):
    ckv = _dot(h_ref[...], wc_ref[...])
    y = ckv * lax.rsqrt(jnp.mean(ckv * ckv, axis=-1, keepdims=True) + RMS_EPS) * g_ref[...]
    yb = y.astype(BF16)
    k = _dot(yb, wk_ref[...])
    k_ref[...] = _rope_cols(k, cos_ref[...], sin_ref[...], HEAD_DIM // 2).astype(k_ref.dtype)
    v_ref[...] = _dot(yb, wv_ref[...]).astype(v_ref.dtype)


def _kv_proj(h, w_ckv, g_kv, w_k, w_v, cos, sin):
    t, d = h.shape
    tm = _tile(t, 512)
    full = lambda a: pl.BlockSpec(a.shape, lambda i: (0, 0))
    row = lambda width: pl.BlockSpec((tm, width), lambda i: (i, 0))
    g2 = g_kv.reshape(1, KV_RANK)
    return pl.pallas_call(
        _kv_kernel,
        grid=(t // tm,),
        in_specs=[row(d), full(w_ckv), full(g2), full(w_k), full(w_v), row(LANES), row(LANES)],
        out_specs=[row(A_Q_WIDTH), row(A_Q_WIDTH)],
        out_shape=[jax.ShapeDtypeStruct((t, A_Q_WIDTH), BF16)] * 2,
        compiler_params=_params("parallel"),
        name="kv_latent",
    )(h, w_ckv, g2, w_k, w_v, cos, sin)


def _idx_kernel(qi_ref, ki_ref, w_ref, o_ref, key_ref, *, tq, ck, n_sel, nkc, scale):
    qb = pl.program_id(1)
    nchunk = (qb * tq + tq + ck - 1) // ck
    q_idx = qb * tq + lax.broadcasted_iota(jnp.int32, (ck, tq), 1)
    s_loc = lax.broadcasted_iota(jnp.int32, (ck, tq), 0)

    def score_chunk(c, carry):
        off = pl.multiple_of(c * ck, ck)
        k = ki_ref[pl.ds(off, ck), :]
        acc = jnp.zeros((ck, tq), F32)
        for h in range(IDX_HEADS):
            d = _dot_nt(k, qi_ref[:, h * IDX_DIM:(h + 1) * IDX_DIM])
            acc = acc + jnp.maximum(d, 0.0) * w_ref[h:h + 1, :]
        bits = pltpu.bitcast(acc * scale, jnp.int32)
        key = jnp.where(bits < 0, bits ^ jnp.int32(0x7FFFFFFF), bits)
        key_ref[pl.ds(off, ck), :] = jnp.where(off + s_loc <= q_idx, key, jnp.int32(INT_MIN))
        return carry

    lax.fori_loop(0, nchunk, score_chunk, 0)

    def bit_step(i, prefix):
        cand_u = prefix | lax.shift_left(jnp.int32(1), 31 - i)
        cand = cand_u ^ jnp.int32(INT_MIN)

        def count_chunk(c, cnt):
            off = pl.multiple_of(c * ck, ck)
            ge = key_ref[pl.ds(off, ck), :] >= cand
            return cnt + jnp.sum(ge.astype(jnp.int32), axis=0, keepdims=True)

        cnt = lax.fori_loop(0, nchunk, count_chunk, jnp.zeros((1, tq), jnp.int32))
        return jnp.where(cnt >= n_sel, cand_u, prefix)

    prefix = lax.fori_loop(0, 32, bit_step, jnp.zeros((1, tq), jnp.int32))
    thr = jnp.maximum(prefix ^ jnp.int32(INT_MIN), jnp.int32(INT_MIN + 1))

    def write_chunk(c, carry):
        off = pl.multiple_of(c * ck, ck)
        sel = key_ref[pl.ds(off, ck), :] >= thr
        o_ref[c] = jnp.where(sel, 0.0, NEG).astype(F32).T.astype(o_ref.dtype)
        return carry

    lax.fori_loop(0, nchunk, write_chunk, 0)

    def fill_chunk(c, carry):
        o_ref[c] = jnp.full((tq, ck), NEG, o_ref.dtype)
        return carry

    lax.fori_loop(nchunk, nkc, fill_chunk, 0)


def _index_select(qi, ki, w_t, n_sel, ck):
    bsz, seq, _ = qi.shape
    tq = _tile(seq, LANES)
    nkc = seq // ck
    scale = (IDX_DIM ** -0.5) * (IDX_HEADS ** -0.5)
    return pl.pallas_call(
        functools.partial(_idx_kernel, tq=tq, ck=ck, n_sel=n_sel, nkc=nkc, scale=scale),
        grid=(bsz, seq // tq),
        in_specs=[
            pl.BlockSpec((None, tq, IDX_Q_WIDTH), lambda b, q: (b, q, 0)),
            pl.BlockSpec((None, seq, IDX_DIM), lambda b, q: (b, 0, 0)),
            pl.BlockSpec((None, IDX_HEADS, tq), lambda b, q: (b, 0, q)),
        ],
        out_specs=pl.BlockSpec((None, nkc, tq, ck), lambda b, q: (b, 0, q, 0)),
        out_shape=jax.ShapeDtypeStruct((bsz, nkc, seq, ck), BF16),
        scratch_shapes=[pltpu.VMEM((seq, tq), jnp.int32)],
        compiler_params=_params("parallel", "parallel"),
        name="indexer_select",
    )(qi, ki, w_t)


def _dsa_kernel(q_ref, k_ref, v_ref, b_ref, o_ref, m_ref, l_ref, acc_ref, *, tq, tk, scale):
    qb = pl.program_id(1)
    kb = pl.program_id(2)
    last = ((qb + 1) * tq - 1) // tk

    @pl.when(kb == 0)
    def _():
        m_ref[...] = jnp.full(m_ref.shape, NEG, F32)
        l_ref[...] = jnp.zeros(l_ref.shape, F32)
        acc_ref[...] = jnp.zeros(acc_ref.shape, F32)

    @pl.when(kb <= last)
    def _():
        bias = b_ref[...].astype(F32)
        for h in range(A_HEADS):
            sl = slice(h * HEAD_DIM, (h + 1) * HEAD_DIM)
            s = _dot_nt(q_ref[:, sl], k_ref[:, sl]) * scale + bias
            m_prev = m_ref[h]
            m_new = jnp.maximum(m_prev, jnp.max(s, axis=1, keepdims=True))
            alpha = jnp.exp(m_prev - m_new)
            p = jnp.exp(s - m_new[:, :1])
            l_ref[h] = alpha * l_ref[h] + jnp.sum(p, axis=1, keepdims=True)
            m_ref[h] = m_new
            acc_ref[:, sl] = acc_ref[:, sl] * alpha + _dot(p.astype(BF16), v_ref[:, sl])

    @pl.when(kb == last)
    def _():
        for h in range(A_HEADS):
            sl = slice(h * HEAD_DIM, (h + 1) * HEAD_DIM)
            o_ref[:, sl] = (acc_ref[:, sl] / l_ref[h]).astype(o_ref.dtype)


def _dsa_attention(q, k, v, bias, tk):
    bsz, seq, width = q.shape
    tq = _tile(seq, 256)
    last = lambda qb: ((qb + 1) * tq - 1) // tk
    kv_spec = pl.BlockSpec((None, tk, width), lambda b, qb, kb: (b, jnp.minimum(kb, last(qb)), 0))
    return pl.pallas_call(
        functools.partial(_dsa_kernel, tq=tq, tk=tk, scale=HEAD_DIM ** -0.5),
        grid=(bsz, seq // tq, seq // tk),
        in_specs=[
            pl.BlockSpec((None, tq, width), lambda b, qb, kb: (b, qb, 0)),
            kv_spec,
            kv_spec,
            pl.BlockSpec((None, None, tq, tk), lambda b, qb, kb: (b, jnp.minimum(kb, last(qb)), qb, 0)),
        ],
        out_specs=pl.BlockSpec((None, tq, width), lambda b, qb, kb: (b, qb, 0)),
        out_shape=jax.ShapeDtypeStruct((bsz, seq, width), BF16),
        scratch_shapes=[
            pltpu.VMEM((A_HEADS, tq, LANES), F32),
            pltpu.VMEM((A_HEADS, tq, LANES), F32),
            pltpu.VMEM((tq, width), F32),
        ],
        compiler_params=_params("parallel", "parallel", "arbitrary"),
        name="dsa_flash",
    )(q, k, v, bias)


def _dil_kernel(q_ref, kc_ref, kp_ref, vc_ref, vp_ref, o_ref, lse_ref, *, w, scale):
    i = pl.program_id(2)
    row = lax.broadcasted_iota(jnp.int32, (w, w), 0)
    col = lax.broadcasted_iota(jnp.int32, (w, w), 1)
    prev_ok = (col >= row) & (i > 0)
    cur_ok = col <= row
    for hh in range(B_HEADS_PER_GROUP):
        sl = slice(hh * HEAD_DIM, (hh + 1) * HEAD_DIM)
        q = q_ref[:, sl]
        sp = jnp.where(prev_ok, _dot_nt(q, kp_ref[:, sl]) * scale, NEG)
        sc = jnp.where(cur_ok, _dot_nt(q, kc_ref[:, sl]) * scale, NEG)
        m = jnp.maximum(jnp.max(sp, axis=1, keepdims=True), jnp.max(sc, axis=1, keepdims=True))
        pp = jnp.exp(sp - m)
        pc = jnp.exp(sc - m)
        l = jnp.sum(pp, axis=1, keepdims=True) + jnp.sum(pc, axis=1, keepdims=True)
        o = _dot(pp.astype(BF16), vp_ref[:, sl]) + _dot(pc.astype(BF16), vc_ref[:, sl])
        o_ref[:, sl] = o / l
        lse_ref[:, sl] = jnp.broadcast_to(m + jnp.log(l), (w, HEAD_DIM))


def _dilated_group(qkv, g, window, dil):
    bsz, seq, width = qkv.shape
    w = window // dil
    sub = seq // dil
    assert sub % w == 0
    nblk = width // B_OUT_WIDTH
    view = qkv.reshape(bsz, sub, dil * width)
    col = lambda which: (lambda b, r, i: (b, i, r * nblk + which * B_GROUPS + g))
    colp = lambda which: (lambda b, r, i: (b, jnp.maximum(i - 1, 0), r * nblk + which * B_GROUPS + g))
    blk = lambda f: pl.BlockSpec((None, w, B_OUT_WIDTH), f)
    out_spec = pl.BlockSpec((None, w, B_OUT_WIDTH), lambda b, r, i: (b, i, r))
    o, lse = pl.pallas_call(
        functools.partial(_dil_kernel, w=w, scale=HEAD_DIM ** -0.5),
        grid=(bsz, dil, sub // w),
        in_specs=[blk(col(0)), blk(col(1)), blk(colp(1)), blk(col(2)), blk(colp(2))],
        out_specs=[out_spec, out_spec],
        out_shape=[jax.ShapeDtypeStruct((bsz, sub, dil * B_OUT_WIDTH), F32)] * 2,
        compiler_params=_params("parallel", "parallel", "parallel"),
        name=f"dilated_g{g}",
    )(view, view, view, view, view)
    return o.reshape(bsz * seq, B_OUT_WIDTH), lse.reshape(bsz * seq, B_OUT_WIDTH)


def _merge_kernel(oa_ref, o0, o1, o2, l0, l1, l2, ga_ref, gb_ref, wa_ref, wb_ref, out_ref):
    m = jnp.maximum(jnp.maximum(l0[...], l1[...]), l2[...])
    e0 = jnp.exp(l0[...] - m)
    e1 = jnp.exp(l1[...] - m)
    e2 = jnp.exp(l2[...] - m)
    ob = (e0 * o0[...] + e1 * o1[...] + e2 * o2[...]) / (e0 + e1 + e2)
    pa = _dot(oa_ref[...], wa_ref[...])
    pb = _dot(ob.astype(BF16), wb_ref[...])
    out_ref[...] = (ga_ref[...].astype(F32) * pa + gb_ref[...].astype(F32) * pb).astype(out_ref.dtype)


def _merge(o_a, o_b, lse_b, gates, w_pa, w_pb):
    t = o_a.shape[0]
    d = w_pa.shape[1]
    tm = _tile(t, 256)
    row = lambda width, cb=0: pl.BlockSpec((tm, width), lambda i: (i, cb))
    full = lambda a: pl.BlockSpec(a.shape, lambda i: (0, 0))
    return pl.pallas_call(
        _merge_kernel,
        grid=(t // tm,),
        in_specs=[row(A_Q_WIDTH)] + [row(B_OUT_WIDTH)] * 6 + [row(d, 0), row(d, 1), full(w_pa), full(w_pb)],
        out_specs=row(d),
        out_shape=jax.ShapeDtypeStruct((t, d), BF16),
        compiler_params=_params("parallel"),
        name="branch_merge",
    )(o_a, *o_b, *lse_b, gates, gates, w_pa, w_pb)


def _router_kernel(h_ref, w_ref, b_ref, idx_ref, wt_ref, *, n_exp):
    lg = _dot(_load_slabs(h_ref, BF16), w_ref[...]) + b_ref[...]
    lane = lax.broadcasted_iota(jnp.int32, lg.shape, 1)
    lg = jnp.where(lane < n_exp, lg, -jnp.inf)
    vals, idxs = [], []
    for _ in range(TOP_K):
        m = jnp.max(lg, axis=1, keepdims=True)
        idx = jnp.min(jnp.where(lg == m, lane, LANES), axis=1, keepdims=True)
        vals.append(m)
        idxs.append(idx)
        lg = jnp.where(lane == idx, -jnp.inf, lg)
    es = [jnp.exp(v - vals[0]) for v in vals]
    tot = es[0] + es[1] + es[2] + es[3]
    idx_out = jnp.zeros(lg.shape, jnp.int32)
    wt_out = jnp.zeros(lg.shape, F32)
    for k in range(TOP_K):
        idx_out = jnp.where(lane == k, idxs[k], idx_out)
        wt_out = jnp.where(lane == k, es[k] / tot, wt_out)
    idx_ref[...] = idx_out
    wt_ref[...] = wt_out


def _router(h, w_r, b_r, n_exp):
    t, _, c = h.shape
    tm = _tile(t, 512)
    row = lambda width: pl.BlockSpec((tm, width), lambda i: (i, 0))
    full = lambda a: pl.BlockSpec(a.shape, lambda i: (0, 0))
    return pl.pallas_call(
        functools.partial(_router_kernel, n_exp=n_exp),
        grid=(t // tm,),
        in_specs=[pl.BlockSpec((tm, SLAB_ROWS, c), lambda i: (i, 0, 0)), full(w_r), full(b_r)],
        out_specs=[row(LANES), row(LANES)],
        out_shape=[jax.ShapeDtypeStruct((t, LANES), jnp.int32), jax.ShapeDtypeStruct((t, LANES), F32)],
        compiler_params=_params("parallel"),
        name="router_top4",
    )(h, w_r, b_r)


def _row_copy(src_hbm, dst_hbm, sem, src_row, dst_row):
    return pltpu.make_async_copy(src_hbm.at[src_row], dst_hbm.at[dst_row], sem)


def _gather_kernel(idx_ref, src_hbm, dst_hbm, sem, *, rows):
    base = pl.program_id(0) * rows

    def issue(r, carry):
        _row_copy(src_hbm, dst_hbm, sem, idx_ref[0, r], base + r).start()
        return carry

    lax.fori_loop(0, rows, issue, 0)

    def drain(r, carry):
        _row_copy(src_hbm, dst_hbm, sem, 0, base + r).wait()
        return carry

    lax.fori_loop(0, rows, drain, 0)


def _gather_rows(src, row_token, rows):
    n_rows = row_token.shape[0]
    steps = n_rows // rows
    return pl.pallas_call(
        functools.partial(_gather_kernel, rows=rows),
        grid=(steps,),
        in_specs=[
            pl.BlockSpec((None, 1, rows), lambda i: (i, 0, 0), memory_space=pltpu.SMEM),
            pl.BlockSpec(memory_space=pl.ANY),
        ],
        out_specs=pl.BlockSpec(memory_space=pl.ANY),
        out_shape=jax.ShapeDtypeStruct((n_rows,) + src.shape[1:], src.dtype),
        scratch_shapes=[pltpu.SemaphoreType.DMA(())],
        compiler_params=_params("arbitrary"),
        name="moe_gather",
    )(row_token.reshape(steps, 1, rows), src)


def _expert_kernel(te_ref, tv_ref, x_ref, wg_ref, wl_ref, bg_ref, bl_ref, w2_ref, b2_ref, y_ref, xb_ref, acc_ref):
    i = pl.program_id(0)
    j = pl.program_id(1)
    nj = pl.num_programs(1)
    valid = tv_ref[i] > 0

    @pl.when(valid)
    def _():
        @pl.when(j == 0)
        def _():
            acc_ref[...] = jnp.zeros(acc_ref.shape, F32)
            xb_ref[...] = _load_slabs(x_ref, BF16)

        x = xb_ref[...]
        glu = jnp.minimum(_dot(x, wg_ref[...]) + bg_ref[...], SWIGLU_LIMIT)
        lin = jnp.clip(_dot(x, wl_ref[...]) + bl_ref[...], -SWIGLU_LIMIT, SWIGLU_LIMIT)
        act = glu * jax.nn.sigmoid(SWIGLU_ALPHA * glu) * (lin + 1.0)
        acc_ref[...] += _dot(act.astype(BF16), w2_ref[...])

        @pl.when(j == nj - 1)
        def _():
            _store_slabs(y_ref, acc_ref[...] + b2_ref[...])

    @pl.when(jnp.logical_not(valid) & (j == nj - 1))
    def _():
        y_ref[...] = jnp.zeros(y_ref.shape, F32)


def _expert_ffn(xs, tile_expert, tile_valid, w1g, w1l, b1g, b1l, w2, b2, tm):
    n_rows, _, c = xs.shape
    n_exp, d, f = w1g.shape
    fc = _tile(f, 512)
    nj = f // fc
    jj = lambda j, tv, i: jnp.where(tv[i] > 0, j, nj - 1)
    grid_spec = pltpu.PrefetchScalarGridSpec(
        num_scalar_prefetch=2,
        grid=(n_rows // tm, nj),
        in_specs=[
            pl.BlockSpec((tm, SLAB_ROWS, c), lambda i, j, te, tv: (i, 0, 0)),
            pl.BlockSpec((None, d, fc), lambda i, j, te, tv: (te[i], 0, jj(j, tv, i))),
            pl.BlockSpec((None, d, fc), lambda i, j, te, tv: (te[i], 0, jj(j, tv, i))),
            pl.BlockSpec((None, 1, fc), lambda i, j, te, tv: (te[i], 0, jj(j, tv, i))),
            pl.BlockSpec((None, 1, fc), lambda i, j, te, tv: (te[i], 0, jj(j, tv, i))),
            pl.BlockSpec((None, fc, d), lambda i, j, te, tv: (te[i], jj(j, tv, i), 0)),
            pl.BlockSpec((None, 1, d), lambda i, j, te, tv: (te[i], 0, 0)),
        ],
        out_specs=pl.BlockSpec((tm, SLAB_ROWS, c), lambda i, j, te, tv: (i, 0, 0)),
        scratch_shapes=[pltpu.VMEM((tm, d), BF16), pltpu.VMEM((tm, d), F32)],
    )
    return pl.pallas_call(
        _expert_kernel,
        grid_spec=grid_spec,
        out_shape=jax.ShapeDtypeStruct((n_rows, SLAB_ROWS, c), F32),
        compiler_params=_params("arbitrary", "arbitrary"),
        name="moe_experts",
    )(tile_expert, tile_valid, xs, w1g, w1l, b1g, b1l, w2, b2)


def _combine_kernel(pos_ref, x_ref, gt_ref, wt_ref, y_hbm, o_ref, buf_ref, sem, *, tc):
    def issue(r, carry):
        for k in range(TOP_K):
            _row_copy(y_hbm, buf_ref.at[k], sem, pos_ref[0, r * TOP_K + k], r).start()
        return carry

    lax.fori_loop(0, tc, issue, 0)

    def drain(r, carry):
        for k in range(TOP_K):
            _row_copy(y_hbm, buf_ref.at[k], sem, 0, r).wait()
        return carry

    lax.fori_loop(0, tc, drain, 0)
    wt = wt_ref[...]
    c = buf_ref.shape[3]
    for s in range(SLAB_ROWS):
        sl = slice(s * c, (s + 1) * c)
        out = jnp.zeros((tc, c), F32)
        for k in range(TOP_K):
            out = out + wt[:, k:k + 1] * buf_ref[k, :, s, :]
        o_ref[:, sl] = x_ref[:, sl] + gt_ref[:, sl] * out


def _combine(x2, gt, wts, pos, y_sorted, seq):
    t, d = x2.shape
    tc = _tile(seq, 128)
    per_b = seq // tc
    steps = t // tc
    return pl.pallas_call(
        functools.partial(_combine_kernel, tc=tc),
        grid=(steps,),
        in_specs=[
            pl.BlockSpec((None, 1, tc * TOP_K), lambda i: (i, 0, 0), memory_space=pltpu.SMEM),
            pl.BlockSpec((tc, d), lambda i: (i, 0)),
            pl.BlockSpec((None, 1, d), lambda i: (i // per_b, 0, 0)),
            pl.BlockSpec((tc, LANES), lambda i: (i, 0)),
            pl.BlockSpec(memory_space=pl.ANY),
        ],
        out_specs=pl.BlockSpec((tc, d), lambda i: (i, 0)),
        out_shape=jax.ShapeDtypeStruct((t, d), F32),
        scratch_shapes=[pltpu.VMEM((TOP_K, tc) + y_sorted.shape[1:], F32), pltpu.SemaphoreType.DMA(())],
        compiler_params=_params("arbitrary"),
        name="moe_combine",
    )(pos.reshape(steps, 1, tc * TOP_K), x2, gt, wts, y_sorted)


def _routing_plan(top_idx, n_exp, tm):
    t = top_idx.shape[0]
    pairs = t * TOP_K
    e_flat = top_idx.reshape(pairs)
    onehot = (e_flat[:, None] == jnp.arange(n_exp, dtype=jnp.int32)[None, :]).astype(jnp.int32)
    csum = jnp.cumsum(onehot, axis=0)
    rank = jnp.sum(csum * onehot, axis=1) - 1
    counts = csum[-1]
    tiles_e = (counts + tm - 1) // tm
    tile_end = jnp.cumsum(tiles_e)
    tile_start = tile_end - tiles_e
    pos = tile_start[e_flat] * tm + rank
    n_tiles = pairs // tm + n_exp
    row_token = jnp.zeros((n_tiles * tm,), jnp.int32).at[pos].set(jnp.arange(pairs, dtype=jnp.int32) // TOP_K)
    tile_ids = jnp.arange(n_tiles, dtype=jnp.int32)
    tile_expert = jnp.minimum(jnp.searchsorted(tile_end, tile_ids, side="right"), n_exp - 1).astype(jnp.int32)
    tile_valid = (tile_ids < tile_end[-1]).astype(jnp.int32)
    last_valid = jnp.maximum(tile_end[-1] - 1, 0)
    tile_expert = jnp.where(tile_valid > 0, tile_expert, tile_expert[last_valid])
    return pos.astype(jnp.int32), row_token, tile_expert, tile_valid


def _rope_tables(positions, dim):
    inv_freq = 1.0 / (ROPE_THETA ** (jnp.arange(0, dim, 2, dtype=F32) / dim))
    ang = positions.astype(F32).reshape(-1, 1) * inv_freq
    return jnp.cos(ang), jnp.sin(ang)


def kernel(x, c, positions, w_mod, b_mod, g_norm1, g_norm2, w_in, g_kv, w_kv_up, w_proj_a, w_proj_b, w_out,
           w_router, b_router, w_moe1, b_moe1, w_moe2, b_moe2, g_final):
    bsz, seq, d = x.shape
    depth = w_mod.shape[0]
    n_exp = w_router.shape[-1]
    t = bsz * seq
    n_sel = min(TOPK_MAX, seq // 4)
    ck = _tile(seq, 512)
    tm_moe = 512 if (t * TOP_K) % 512 == 0 else 64

    cos, sin = _rope_tables(positions, HEAD_DIM)
    cos_a = jnp.concatenate([cos, cos], axis=1)
    sin_a = jnp.concatenate([-sin, sin], axis=1)
    cos_i, sin_i = _rope_tables(positions, IDX_DIM)
    cos_q = jnp.tile(cos_i, (1, LANES // (IDX_DIM // 2)))
    sin_q = jnp.tile(jnp.concatenate([-sin_i, sin_i], axis=1), (1, LANES // IDX_DIM))
    lane_is_k = (jnp.arange(LANES) < IDX_DIM)[None, :]
    cos_k = jnp.where(lane_is_k, cos_q, 1.0)
    sin_k = jnp.where(lane_is_k, sin_q, 0.0)

    mod = _modulation(c, w_mod, b_mod)
    x2 = x.reshape(t, d)

    o1, o2, o3, o4, o5, o6 = (A_Q_WIDTH, A_Q_WIDTH + KV_RANK, A_Q_WIDTH + KV_RANK + IDX_Q_WIDTH,
                              A_Q_WIDTH + KV_RANK + IDX_Q_WIDTH + IDX_DIM,
                              A_Q_WIDTH + KV_RANK + IDX_Q_WIDTH + IDX_DIM + IDX_HEADS,
                              A_Q_WIDTH + KV_RANK + IDX_Q_WIDTH + IDX_DIM + IDX_HEADS + B_QKV_WIDTH)

    for l in range(depth):
        mvec = lambda n: mod[l, :, n * d:(n + 1) * d].reshape(bsz, 1, d)
        sh1, sc1, gt1, sh2, sc2, gt2 = (mvec(n) for n in range(N_MOD))

        wl = w_in[l]
        w_qa = wl[:, :o1].astype(BF16)
        w_ckv = wl[:, o1:o2].astype(BF16)
        w_qi = wl[:, o2:o3].astype(BF16)
        w_kw = jnp.pad(wl[:, o3:o5], ((0, 0), (0, LANES - (o5 - o3)))).astype(BF16)
        w_qkvb = wl[:, o5:o6].astype(BF16)
        w_gate = wl[:, o6:].astype(BF16)
        wkv = w_kv_up[l].reshape(KV_RANK, A_HEADS, 2, HEAD_DIM)
        w_k = wkv[:, :, 0].reshape(KV_RANK, A_Q_WIDTH).astype(BF16)
        w_v = wkv[:, :, 1].reshape(KV_RANK, A_Q_WIDTH).astype(BF16)

        h = _norm(x2, g_norm1[l], seq, sh1, sc1)
        qa = _mm(h, w_qa, epi="rope64", tables=(cos_a, sin_a))
        ka, va = _kv_proj(h, w_ckv, g_kv[l], w_k, w_v, cos_a, sin_a)
        qi = _mm(h, w_qi, epi="rope32", tables=(cos_q, sin_q))
        kw = _mm(h, w_kw, epi="rope32", tables=(cos_k, sin_k), out_dtype=F32)
        ki = kw[:, :IDX_DIM].astype(BF16).reshape(bsz, seq, IDX_DIM)
        w_t = jnp.swapaxes(kw[:, IDX_DIM:IDX_DIM + IDX_HEADS].reshape(bsz, seq, IDX_HEADS), 1, 2)
        n_rope_tiles = (2 * B_GROUPS * B_OUT_WIDTH) // _tile(B_QKV_WIDTH, 512)
        qkvb = _mm(h, w_qkvb, epi="rope64", tables=(cos_a, sin_a), rope_cols=n_rope_tiles)
        gates = _mm(h, w_gate, epi="sigmoid")

        bias = _index_select(qi.reshape(bsz, seq, IDX_Q_WIDTH), ki, w_t, n_sel, ck)
        o_a = _dsa_attention(qa.reshape(bsz, seq, A_Q_WIDTH), ka.reshape(bsz, seq, A_Q_WIDTH),
                             va.reshape(bsz, seq, A_Q_WIDTH), bias, ck)
        qkvb3 = qkvb.reshape(bsz, seq, B_QKV_WIDTH)
        o_b, lse_b = zip(*[_dilated_group(qkvb3, g, win, dil) for g, (win, dil) in enumerate(B_PATTERNS)])
        merged = _merge(o_a.reshape(t, A_Q_WIDTH), o_b, lse_b, gates,
                        w_proj_a[l].astype(BF16), w_proj_b[l].astype(BF16))
        x2 = _mm(merged, w_out[l].astype(BF16), epi="resid", resid=(x2, gt1), seq=seq, out_dtype=F32,
                 tm_pref=512)

        h2 = _norm(x2, g_norm2[l], seq, sh2, sc2, out_dtype=F32, slab=True)
        w_r = jnp.pad(w_router[l], ((0, 0), (0, LANES - n_exp))).astype(BF16)
        b_r = jnp.pad(b_router[l], (0, LANES - n_exp)).reshape(1, LANES)
        top_idx, top_w = _router(h2, w_r, b_r, n_exp)
        pos, row_token, tile_expert, tile_valid = _routing_plan(top_idx[:, :TOP_K], n_exp, tm_moe)
        xs = _gather_rows(h2, row_token, _tile(row_token.shape[0], 256))
        w1 = w_moe1[l]
        f = w1.shape[-1] // 2
        y_sorted = _expert_ffn(
            xs, tile_expert, tile_valid,
            w1[..., 0::2].astype(BF16), w1[..., 1::2].astype(BF16),
            b_moe1[l][:, 0::2].reshape(n_exp, 1, f), b_moe1[l][:, 1::2].reshape(n_exp, 1, f),
            w_moe2[l].astype(BF16), b_moe2[l].reshape(n_exp, 1, d), tm_moe)
        x2 = _combine(x2, gt2, top_w, pos, y_sorted, seq)

    return _norm(x2, g_final, seq, out_dtype=x.dtype).reshape(bsz, seq, d)
```

```python
import functools

import jax
import jax.numpy as jnp
from jax import lax
from jax.experimental import pallas as pl
from jax.experimental.pallas import tpu as pltpu

HEAD_DIM = 128
ROPE_THETA = 10000.0
RMS_EPS = 1e-5
A_HEADS = 8
KV_RANK = 512
IDX_HEADS = 16
IDX_DIM = 64
TOPK_MAX = 256
B_PATTERNS = ((128, 1), (512, 4), (2048, 16))
B_GROUPS = 3
B_HEADS_PER_GROUP = 4
N_MOD = 6
TOP_K = 4
SWIGLU_LIMIT = 7.0
SWIGLU_ALPHA = 1.702

LANES = 128
NEG = -1e30
INT_MIN = -(2 ** 31)
VMEM_LIMIT = 48 * 1024 * 1024
GATHER_UNROLL = 8

A_Q_WIDTH = A_HEADS * HEAD_DIM
IDX_Q_WIDTH = IDX_HEADS * IDX_DIM
B_OUT_WIDTH = B_HEADS_PER_GROUP * HEAD_DIM
B_QKV_WIDTH = 3 * B_GROUPS * B_OUT_WIDTH

F32 = jnp.float32
BF16 = jnp.bfloat16
NT_DIMS = (((1,), (1,)), ((), ()))


def _tile(n, pref):
    return pref if n % pref == 0 else n


def _params(*sem):
    return pltpu.CompilerParams(dimension_semantics=sem, vmem_limit_bytes=VMEM_LIMIT)


def _dot(a, b):
    return jnp.dot(a, b, preferred_element_type=F32)


def _dot_nt(a, b):
    return lax.dot_general(a, b, NT_DIMS, preferred_element_type=F32)


def _mod_kernel(c_ref, w_ref, b_ref, o_ref):
    c = c_ref[...]
    c_act = c * jax.nn.sigmoid(c)
    o_ref[...] = _dot(c_act, w_ref[...]) + b_ref[...]


def _modulation(c, w_mod, b_mod):
    depth, d, n = w_mod.shape
    bsz = c.shape[0]
    tn = _tile(n, 1024)
    return pl.pallas_call(
        _mod_kernel,
        grid=(depth, n // tn),
        in_specs=[
            pl.BlockSpec((bsz, d), lambda l, j: (0, 0)),
            pl.BlockSpec((None, d, tn), lambda l, j: (l, 0, j)),
            pl.BlockSpec((None, 1, tn), lambda l, j: (l, 0, j)),
        ],
        out_specs=pl.BlockSpec((None, bsz, tn), lambda l, j: (l, 0, j)),
        out_shape=jax.ShapeDtypeStruct((depth, bsz, n), F32),
        compiler_params=_params("parallel", "parallel"),
        name="adaln_mod",
    )(c, w_mod, b_mod.reshape(depth, 1, n))


def _store_slabs(ref, y):
    rows, ns = y.shape[0], y.shape[1] // LANES
    for s in range(ns):
        ref[pl.ds(s, rows, stride=ns), :] = y[:, s * LANES:(s + 1) * LANES].astype(ref.dtype)


def _load_slabs(ref, rows, dtype):
    ns = ref.shape[0] // rows
    return jnp.concatenate([ref[pl.ds(s, rows, stride=ns), :].astype(dtype) for s in range(ns)], axis=1)


def _norm_kernel(x_ref, g_ref, *rest, modulate, slab):
    x = x_ref[...]
    y = x * lax.rsqrt(jnp.mean(x * x, axis=-1, keepdims=True) + RMS_EPS) * g_ref[...]
    if modulate:
        sh_ref, sc_ref, o_ref = rest
        y = y * (1.0 + sc_ref[...]) + sh_ref[...]
    else:
        (o_ref,) = rest
    if slab:
        _store_slabs(o_ref, y)
    else:
        o_ref[...] = y.astype(o_ref.dtype)


def _norm(x2, g, seq, shift=None, scale=None, out_dtype=BF16, slab=False):
    t, d = x2.shape
    tm = _tile(seq, 512)
    per_b = seq // tm
    in_specs = [pl.BlockSpec((tm, d), lambda i: (i, 0)), pl.BlockSpec((1, d), lambda i: (0, 0))]
    args = [x2, g.reshape(1, d)]
    if shift is not None:
        vec = pl.BlockSpec((None, 1, d), lambda i: (i // per_b, 0, 0))
        in_specs += [vec, vec]
        args += [shift, scale]
    if slab:
        ns = d // LANES
        out_spec = pl.BlockSpec((tm * ns, LANES), lambda i: (i, 0))
        out_shape = jax.ShapeDtypeStruct((t * ns, LANES), out_dtype)
    else:
        out_spec = pl.BlockSpec((tm, d), lambda i: (i, 0))
        out_shape = jax.ShapeDtypeStruct((t, d), out_dtype)
    return pl.pallas_call(
        functools.partial(_norm_kernel, modulate=shift is not None, slab=slab),
        grid=(t // tm,),
        in_specs=in_specs,
        out_specs=out_spec,
        out_shape=out_shape,
        compiler_params=_params("parallel"),
        name="rmsnorm_mod",
    )(*args)


def _rope_cols(x, cos, sin, half):
    outs = []
    for c in range(x.shape[1] // LANES):
        xc = x[:, c * LANES:(c + 1) * LANES]
        if half == LANES // 2:
            partner = pltpu.roll(xc, LANES // 2, 1)
        else:
            lane = lax.broadcasted_iota(jnp.int32, xc.shape, 1)
            low = (lane & (2 * half - 1)) < half
            partner = jnp.where(low, pltpu.roll(xc, LANES - half, 1), pltpu.roll(xc, half, 1))
        outs.append(xc * cos + partner * sin)
    return outs[0] if len(outs) == 1 else jnp.concatenate(outs, axis=1)


def _mm_kernel(h_ref, w_ref, *rest, epi, rope_cols):
    acc = _dot(h_ref[...], w_ref[...])
    if epi in ("rope64", "rope32"):
        cos_ref, sin_ref, o_ref = rest
        half = 64 if epi == "rope64" else 32
        if rope_cols is None:
            acc = _rope_cols(acc, cos_ref[...], sin_ref[...], half)
        else:
            j = pl.program_id(1)
            roped = _rope_cols(acc, cos_ref[...], sin_ref[...], half)
            acc = jnp.where(j < rope_cols, roped, acc)
    elif epi == "sigmoid":
        (o_ref,) = rest
        acc = jax.nn.sigmoid(acc)
    elif epi == "resid":
        x_ref, gt_ref, o_ref = rest
        acc = x_ref[...] + gt_ref[...] * acc
    else:
        (o_ref,) = rest
    o_ref[...] = acc.astype(o_ref.dtype)


def _mm(h, w, *, epi="none", tables=None, resid=None, seq=None, out_dtype=BF16, rope_cols=None,
        tm_pref=1024, tn_pref=512):
    t, k = h.shape
    n = w.shape[1]
    tm = _tile(t if seq is None else seq, tm_pref)
    tn = _tile(n, tn_pref)
    in_specs = [pl.BlockSpec((tm, k), lambda i, j: (i, 0)), pl.BlockSpec((k, tn), lambda i, j: (0, j))]
    args = [h, w]
    if tables is not None:
        spec = pl.BlockSpec((tm, LANES), lambda i, j: (i, 0))
        in_specs += [spec, spec]
        args += list(tables)
    if resid is not None:
        x2, gt = resid
        per_b = seq // tm
        in_specs += [pl.BlockSpec((tm, tn), lambda i, j: (i, j)),
                     pl.BlockSpec((None, 1, tn), lambda i, j: (i // per_b, 0, j))]
        args += [x2, gt]
    return pl.pallas_call(
        functools.partial(_mm_kernel, epi=epi, rope_cols=rope_cols),
        grid=(t // tm, n // tn),
        in_specs=in_specs,
        out_specs=pl.BlockSpec((tm, tn), lambda i, j: (i, j)),
        out_shape=jax.ShapeDtypeStruct((t, n), out_dtype),
        compiler_params=_params("parallel", "parallel"),
        name="proj_" + epi,
    )(*args)


def _kv_kernel(h_ref, wc_ref, g_ref, wk_ref, wv_ref, cos_ref, sin_ref, k_ref, v_ref):
    ckv = _dot(h_ref[...], wc_ref[...])
    y = ckv * lax.rsqrt(jnp.mean(ckv * ckv, axis=-1, keepdims=True) + RMS_EPS) * g_ref[...]
    yb = y.astype(BF16)
    k = _dot(yb, wk_ref[...])
    k_ref[...] = _rope_cols(k, cos_ref[...], sin_ref[...], HEAD_DIM // 2).astype(k_ref.dtype)
    v_ref[...] = _dot(yb, wv_ref[...]).astype(v_ref.dtype)


def _kv_proj(h, w_ckv, g_kv, w_k, w_v, cos, sin):
    t, d = h.shape
    tm = _tile(t, 512)
    full = lambda a: pl.BlockSpec(a.shape, lambda i: (0, 0))
    row = lambda width: pl.BlockSpec((tm, width), lambda i: (i, 0))
    g2 = g_kv.reshape(1, KV_RANK)
    return pl.pallas_call(
        _kv_kernel,
        grid=(t // tm,),
        in_specs=[row(d), full(w_ckv), full(g2), full(w_k), full(w_v), row(LANES), row(LANES)],
        out_specs=[row(A_Q_WIDTH), row(A_Q_WIDTH)],
        out_shape=[jax.ShapeDtypeStruct((t, A_Q_WIDTH), BF16)] * 2,
        compiler_params=_params("parallel"),
        name="kv_latent",
    )(h, w_ckv, g2, w_k, w_v, cos, sin)


def _idx_kernel(qi_ref, ki_ref, w_ref, o_ref, key_ref, *, tq, ck, n_sel, nkc, scale):
    qb = pl.program_id(1)
    nchunk = (qb * tq + tq + ck - 1) // ck
    q_idx = qb * tq + lax.broadcasted_iota(jnp.int32, (ck, tq), 1)
    s_loc = lax.broadcasted_iota(jnp.int32, (ck, tq), 0)

    def score_chunk(c, carry):
        off = pl.multiple_of(c * ck, ck)
        k = ki_ref[pl.ds(off, ck), :]
        acc = jnp.zeros((ck, tq), F32)
        for h in range(IDX_HEADS):
            d = _dot_nt(k, qi_ref[:, h * IDX_DIM:(h + 1) * IDX_DIM])
            acc = acc + jnp.maximum(d, 0.0) * w_ref[h:h + 1, :]
        bits = pltpu.bitcast(acc * scale, jnp.int32)
        key = jnp.where(bits < 0, bits ^ jnp.int32(0x7FFFFFFF), bits)
        key_ref[pl.ds(off, ck), :] = jnp.where(off + s_loc <= q_idx, key, jnp.int32(INT_MIN))
        return carry

    lax.fori_loop(0, nchunk, score_chunk, 0)

    def bit_step(i, prefix):
        cand_u = prefix | lax.shift_left(jnp.int32(1), 31 - i)
        cand = cand_u ^ jnp.int32(INT_MIN)

        def count_chunk(c, cnt):
            off = pl.multiple_of(c * ck, ck)
            ge = key_ref[pl.ds(off, ck), :] >= cand
            return cnt + jnp.sum(ge.astype(jnp.int32), axis=0, keepdims=True)

        cnt = lax.fori_loop(0, nchunk, count_chunk, jnp.zeros((1, tq), jnp.int32))
        return jnp.where(cnt >= n_sel, cand_u, prefix)

    prefix = lax.fori_loop(0, 32, bit_step, jnp.zeros((1, tq), jnp.int32))
    thr = jnp.maximum(prefix ^ jnp.int32(INT_MIN), jnp.int32(INT_MIN + 1))

    def write_chunk(c, carry):
        off = pl.multiple_of(c * ck, ck)
        sel = key_ref[pl.ds(off, ck), :] >= thr
        o_ref[c] = jnp.where(sel, 0.0, NEG).astype(F32).T.astype(o_ref.dtype)
        return carry

    lax.fori_loop(0, nchunk, write_chunk, 0)

    def fill_chunk(c, carry):
        o_ref[c] = jnp.full((tq, ck), NEG, o_ref.dtype)
        return carry

    lax.fori_loop(nchunk, nkc, fill_chunk, 0)


def _index_select(qi, ki, w_t, n_sel, ck):
    bsz, seq, _ = qi.shape
    tq = _tile(seq, LANES)
    nkc = seq // ck
    scale = (IDX_DIM ** -0.5) * (IDX_HEADS ** -0.5)
    return pl.pallas_call(
        functools.partial(_idx_kernel, tq=tq, ck=ck, n_sel=n_sel, nkc=nkc, scale=scale),
        grid=(bsz, seq // tq),
        in_specs=[
            pl.BlockSpec((None, tq, IDX_Q_WIDTH), lambda b, q: (b, q, 0)),
            pl.BlockSpec((None, seq, IDX_DIM), lambda b, q: (b, 0, 0)),
            pl.BlockSpec((None, IDX_HEADS, tq), lambda b, q: (b, 0, q)),
        ],
        out_specs=pl.BlockSpec((None, nkc, tq, ck), lambda b, q: (b, 0, q, 0)),
        out_shape=jax.ShapeDtypeStruct((bsz, nkc, seq, ck), BF16),
        scratch_shapes=[pltpu.VMEM((seq, tq), jnp.int32)],
        compiler_params=_params("parallel", "parallel"),
        name="indexer_select",
    )(qi, ki, w_t)


def _dsa_kernel(q_ref, k_ref, v_ref, b_ref, o_ref, m_ref, l_ref, acc_ref, *, tq, tk, scale):
    qb = pl.program_id(1)
    kb = pl.program_id(2)
    last = ((qb + 1) * tq - 1) // tk

    @pl.when(kb == 0)
    def _():
        m_ref[...] = jnp.full(m_ref.shape, NEG, F32)
        l_ref[...] = jnp.zeros(l_ref.shape, F32)
        acc_ref[...] = jnp.zeros(acc_ref.shape, F32)

    @pl.when(kb <= last)
    def _():
        bias = b_ref[...].astype(F32)
        for h in range(A_HEADS):
            sl = slice(h * HEAD_DIM, (h + 1) * HEAD_DIM)
            s = _dot_nt(q_ref[:, sl], k_ref[:, sl]) * scale + bias
            m_prev = m_ref[h]
            m_new = jnp.maximum(m_prev, jnp.max(s, axis=1, keepdims=True))
            alpha = jnp.exp(m_prev - m_new)
            p = jnp.exp(s - m_new[:, :1])
            l_ref[h] = alpha * l_ref[h] + jnp.sum(p, axis=1, keepdims=True)
            m_ref[h] = m_new
            acc_ref[:, sl] = acc_ref[:, sl] * alpha + _dot(p.astype(BF16), v_ref[:, sl])

    @pl.when(kb == last)
    def _():
        for h in range(A_HEADS):
            sl = slice(h * HEAD_DIM, (h + 1) * HEAD_DIM)
            o_ref[:, sl] = (acc_ref[:, sl] / l_ref[h]).astype(o_ref.dtype)


def _dsa_attention(q, k, v, bias, tk):
    bsz, seq, width = q.shape
    tq = _tile(seq, 256)
    last = lambda qb: ((qb + 1) * tq - 1) // tk
    kv_spec = pl.BlockSpec((None, tk, width), lambda b, qb, kb: (b, jnp.minimum(kb, last(qb)), 0))
    return pl.pallas_call(
        functools.partial(_dsa_kernel, tq=tq, tk=tk, scale=HEAD_DIM ** -0.5),
        grid=(bsz, seq // tq, seq // tk),
        in_specs=[
            pl.BlockSpec((None, tq, width), lambda b, qb, kb: (b, qb, 0)),
            kv_spec,
            kv_spec,
            pl.BlockSpec((None, None, tq, tk), lambda b, qb, kb: (b, jnp.minimum(kb, last(qb)), qb, 0)),
        ],
        out_specs=pl.BlockSpec((None, tq, width), lambda b, qb, kb: (b, qb, 0)),
        out_shape=jax.ShapeDtypeStruct((bsz, seq, width), BF16),
        scratch_shapes=[
            pltpu.VMEM((A_HEADS, tq, LANES), F32),
            pltpu.VMEM((A_HEADS, tq, LANES), F32),
            pltpu.VMEM((tq, width), F32),
        ],
        compiler_params=_params("parallel", "parallel", "arbitrary"),
        name="dsa_flash",
    )(q, k, v, bias)


def _dil_kernel(q_ref, kc_ref, kp_ref, vc_ref, vp_ref, o_ref, lse_ref, *, w, scale):
    i = pl.program_id(2)
    row = lax.broadcasted_iota(jnp.int32, (w, w), 0)
    col = lax.broadcasted_iota(jnp.int32, (w, w), 1)
    prev_ok = (col >= row) & (i > 0)
    cur_ok = col <= row
    for hh in range(B_HEADS_PER_GROUP):
        sl = slice(hh * HEAD_DIM, (hh + 1) * HEAD_DIM)
        q = q_ref[:, sl]
        sp = jnp.where(prev_ok, _dot_nt(q, kp_ref[:, sl]) * scale, NEG)
        sc = jnp.where(cur_ok, _dot_nt(q, kc_ref[:, sl]) * scale, NEG)
        m = jnp.maximum(jnp.max(sp, axis=1, keepdims=True), jnp.max(sc, axis=1, keepdims=True))
        pp = jnp.exp(sp - m)
        pc = jnp.exp(sc - m)
        l = jnp.sum(pp, axis=1, keepdims=True) + jnp.sum(pc, axis=1, keepdims=True)
        o = _dot(pp.astype(BF16), vp_ref[:, sl]) + _dot(pc.astype(BF16), vc_ref[:, sl])
        o_ref[:, sl] = o / l
        lse_ref[:, sl] = jnp.broadcast_to(m + jnp.log(l), (w, HEAD_DIM))


def _dilated_group(qkv, g, window, dil):
    bsz, seq, width = qkv.shape
    w = window // dil
    sub = seq // dil
    assert sub % w == 0
    nblk = width // B_OUT_WIDTH
    view = qkv.reshape(bsz, sub, dil * width)
    col = lambda which: (lambda b, r, i: (b, i, r * nblk + which * B_GROUPS + g))
    colp = lambda which: (lambda b, r, i: (b, jnp.maximum(i - 1, 0), r * nblk + which * B_GROUPS + g))
    blk = lambda f: pl.BlockSpec((None, w, B_OUT_WIDTH), f)
    out_spec = pl.BlockSpec((None, w, B_OUT_WIDTH), lambda b, r, i: (b, i, r))
    o, lse = pl.pallas_call(
        functools.partial(_dil_kernel, w=w, scale=HEAD_DIM ** -0.5),
        grid=(bsz, dil, sub // w),
        in_specs=[blk(col(0)), blk(col(1)), blk(colp(1)), blk(col(2)), blk(colp(2))],
        out_specs=[out_spec, out_spec],
        out_shape=[jax.ShapeDtypeStruct((bsz, sub, dil * B_OUT_WIDTH), F32)] * 2,
        compiler_params=_params("parallel", "parallel", "parallel"),
        name=f"dilated_g{g}",
    )(view, view, view, view, view)
    return o.reshape(bsz * seq, B_OUT_WIDTH), lse.reshape(bsz * seq, B_OUT_WIDTH)


def _merge_kernel(oa_ref, o0, o1, o2, l0, l1, l2, ga_ref, gb_ref, wa_ref, wb_ref, out_ref):
    m = jnp.maximum(jnp.maximum(l0[...], l1[...]), l2[...])
    e0 = jnp.exp(l0[...] - m)
    e1 = jnp.exp(l1[...] - m)
    e2 = jnp.exp(l2[...] - m)
    ob = (e0 * o0[...] + e1 * o1[...] + e2 * o2[...]) / (e0 + e1 + e2)
    pa = _dot(oa_ref[...], wa_ref[...])
    pb = _dot(ob.astype(BF16), wb_ref[...])
    out_ref[...] = (ga_ref[...].astype(F32) * pa + gb_ref[...].astype(F32) * pb).astype(out_ref.dtype)


def _merge(o_a, o_b, lse_b, gates, w_pa, w_pb):
    t = o_a.shape[0]
    d = w_pa.shape[1]
    tm = _tile(t, 256)
    row = lambda width, cb=0: pl.BlockSpec((tm, width), lambda i: (i, cb))
    full = lambda a: pl.BlockSpec(a.shape, lambda i: (0, 0))
    return pl.pallas_call(
        _merge_kernel,
        grid=(t // tm,),
        in_specs=[row(A_Q_WIDTH)] + [row(B_OUT_WIDTH)] * 6 + [row(d, 0), row(d, 1), full(w_pa), full(w_pb)],
        out_specs=row(d),
        out_shape=jax.ShapeDtypeStruct((t, d), BF16),
        compiler_params=_params("parallel"),
        name="branch_merge",
    )(o_a, *o_b, *lse_b, gates, gates, w_pa, w_pb)


def _router_kernel(h_ref, w_ref, b_ref, idx_ref, wt_ref, *, n_exp):
    lg = _dot(_load_slabs(h_ref, idx_ref.shape[0], BF16), w_ref[...]) + b_ref[...]
    lane = lax.broadcasted_iota(jnp.int32, lg.shape, 1)
    lg = jnp.where(lane < n_exp, lg, -jnp.inf)
    vals, idxs = [], []
    for _ in range(TOP_K):
        m = jnp.max(lg, axis=1, keepdims=True)
        idx = jnp.min(jnp.where(lg == m, lane, LANES), axis=1, keepdims=True)
        vals.append(m)
        idxs.append(idx)
        lg = jnp.where(lane == idx, -jnp.inf, lg)
    es = [jnp.exp(v - vals[0]) for v in vals]
    tot = es[0] + es[1] + es[2] + es[3]
    idx_out = jnp.zeros(lg.shape, jnp.int32)
    wt_out = jnp.zeros(lg.shape, F32)
    for k in range(TOP_K):
        idx_out = jnp.where(lane == k, idxs[k], idx_out)
        wt_out = jnp.where(lane == k, es[k] / tot, wt_out)
    idx_ref[...] = idx_out
    wt_ref[...] = wt_out


def _router(h, w_r, b_r, n_exp):
    ns = w_r.shape[0] // LANES
    t = h.shape[0] // ns
    tm = _tile(t, 512)
    row = lambda width: pl.BlockSpec((tm, width), lambda i: (i, 0))
    full = lambda a: pl.BlockSpec(a.shape, lambda i: (0, 0))
    return pl.pallas_call(
        functools.partial(_router_kernel, n_exp=n_exp),
        grid=(t // tm,),
        in_specs=[pl.BlockSpec((tm * ns, LANES), lambda i: (i, 0)), full(w_r), full(b_r)],
        out_specs=[row(LANES), row(LANES)],
        out_shape=[jax.ShapeDtypeStruct((t, LANES), jnp.int32), jax.ShapeDtypeStruct((t, LANES), F32)],
        compiler_params=_params("parallel"),
        name="router_top4",
    )(h, w_r, b_r)


def _row_copy(src_hbm, dst_ref, sem, src_row, dst_row):
    ns = src_hbm.shape[1]
    dst = dst_ref.at[pl.ds(pl.multiple_of(dst_row * ns, ns), ns)]
    return pltpu.make_async_copy(src_hbm.at[src_row], dst, sem)


def _gather_start(idx_ref, src_hbm, dst_ref, sem, n):
    def body(r, carry):
        _row_copy(src_hbm, dst_ref, sem, idx_ref[0, r], r).start()
        return carry

    lax.fori_loop(0, n, body, 0, unroll=GATHER_UNROLL)


def _gather_wait(src_hbm, dst_ref, sem, n):
    def body(r, carry):
        _row_copy(src_hbm, dst_ref, sem, 0, r).wait()
        return carry

    lax.fori_loop(0, n, body, 0, unroll=GATHER_UNROLL)


def _gather_pipelined(i, n_steps, idx_ref, idx_next_ref, src_hbm, buf_ref, sem, n):
    slot = i % 2

    @pl.when(i == 0)
    def _():
        _gather_start(idx_ref, src_hbm, buf_ref.at[0], sem.at[0], n)

    @pl.when(i + 1 < n_steps)
    def _():
        _gather_start(idx_next_ref, src_hbm, buf_ref.at[1 - slot], sem.at[1 - slot], n)

    _gather_wait(src_hbm, buf_ref.at[slot], sem.at[slot], n)
    return slot


def _deint_kernel(w_ref, p_ref, o_ref):
    p = p_ref[...]
    for g in range(w_ref.shape[1] // (2 * LANES)):
        sl = slice(g * 2 * LANES, (g + 1) * 2 * LANES)
        o_ref[:, sl] = _dot(w_ref[:, sl].astype(BF16), p).astype(o_ref.dtype)


def _deinterleave(w):
    n_exp, d, n = w.shape
    tn = _tile(n, 512)
    k = lax.broadcasted_iota(jnp.int32, (2 * LANES, 2 * LANES), 0)
    col = lax.broadcasted_iota(jnp.int32, (2 * LANES, 2 * LANES), 1)
    perm = (k == 2 * (col % LANES) + col // LANES).astype(BF16)
    return pl.pallas_call(
        _deint_kernel,
        grid=(n_exp, n // tn),
        in_specs=[pl.BlockSpec((None, d, tn), lambda e, j: (e, 0, j)),
                  pl.BlockSpec(perm.shape, lambda e, j: (0, 0))],
        out_specs=pl.BlockSpec((None, d, tn), lambda e, j: (e, 0, j)),
        out_shape=jax.ShapeDtypeStruct((n_exp, d, n), BF16),
        compiler_params=_params("parallel", "parallel"),
        name="moe_w1_layout",
    )(w, perm)


def _expert_kernel(te_ref, tv_ref, tok_ref, tok_next_ref, h_hbm, w1_ref, b1_ref, w2_ref, b2_ref, y_ref,
                   xg_ref, xb_ref, acc_ref, sem, *, tm):
    i = pl.program_id(0)
    j = pl.program_id(1)
    nj = pl.num_programs(1)
    valid = tv_ref[i] > 0

    @pl.when(j == 0)
    def _():
        slot = _gather_pipelined(i, pl.num_programs(0), tok_ref, tok_next_ref, h_hbm, xg_ref, sem, tm)
        xb_ref[...] = _load_slabs(xg_ref.at[slot], tm, BF16)
        acc_ref[...] = jnp.zeros(acc_ref.shape, F32)

    @pl.when(valid)
    def _():
        a = _dot(xb_ref[...], w1_ref[...]) + b1_ref[...]
        acts = []
        for g in range(a.shape[1] // (2 * LANES)):
            glu = jnp.minimum(a[:, 2 * g * LANES:(2 * g + 1) * LANES], SWIGLU_LIMIT)
            lin = jnp.clip(a[:, (2 * g + 1) * LANES:(2 * g + 2) * LANES], -SWIGLU_LIMIT, SWIGLU_LIMIT)
            acts.append((glu * jax.nn.sigmoid(SWIGLU_ALPHA * glu) * (lin + 1.0)).astype(BF16))
        act = acts[0] if len(acts) == 1 else jnp.concatenate(acts, axis=1)
        acc_ref[...] += _dot(act, w2_ref[...])

    @pl.when(j == nj - 1)
    def _():
        _store_slabs(y_ref, acc_ref[...] + b2_ref[...])


def _expert_ffn(h_slabs, row_token, tile_expert, tile_valid, w1p, b1p, w2, b2, tm):
    n_rows = row_token.shape[0]
    n_tiles = n_rows // tm
    n_exp, d, f2 = w1p.shape
    ns = d // LANES
    h_slabs = h_slabs.reshape(-1, ns, LANES)
    fc = _tile(f2 // 2, 512)
    nj = (f2 // 2) // fc
    jj = lambda j, tv, i: jnp.where(tv[i] > 0, j, nj - 1)
    tok = row_token.reshape(n_tiles, 1, tm)
    grid_spec = pltpu.PrefetchScalarGridSpec(
        num_scalar_prefetch=2,
        grid=(n_tiles, nj),
        in_specs=[
            pl.BlockSpec((None, 1, tm), lambda i, j, te, tv: (i, 0, 0), memory_space=pltpu.SMEM),
            pl.BlockSpec((None, 1, tm), lambda i, j, te, tv: (jnp.minimum(i + 1, n_tiles - 1), 0, 0),
                         memory_space=pltpu.SMEM),
            pl.BlockSpec(memory_space=pl.ANY),
            pl.BlockSpec((None, d, 2 * fc), lambda i, j, te, tv: (te[i], 0, jj(j, tv, i))),
            pl.BlockSpec((None, 1, 2 * fc), lambda i, j, te, tv: (te[i], 0, jj(j, tv, i))),
            pl.BlockSpec((None, fc, d), lambda i, j, te, tv: (te[i], jj(j, tv, i), 0)),
            pl.BlockSpec((None, 1, d), lambda i, j, te, tv: (te[i], 0, 0)),
        ],
        out_specs=pl.BlockSpec((tm * ns, LANES), lambda i, j, te, tv: (i, 0)),
        scratch_shapes=[pltpu.VMEM((2, tm * ns, LANES), F32), pltpu.VMEM((tm, d), BF16),
                        pltpu.VMEM((tm, d), F32), pltpu.SemaphoreType.DMA((2,))],
    )
    return pl.pallas_call(
        functools.partial(_expert_kernel, tm=tm),
        grid_spec=grid_spec,
        out_shape=jax.ShapeDtypeStruct((n_rows * ns, LANES), F32),
        compiler_params=_params("arbitrary", "arbitrary"),
        name="moe_experts",
    )(tile_expert, tile_valid, tok, tok, h_slabs, w1p, b1p, w2, b2)


def _combine_kernel(pos_ref, pos_next_ref, x_ref, gt_ref, wt_ref, y_hbm, o_ref, buf_ref, sem, *, tc):
    slot = _gather_pipelined(pl.program_id(0), pl.num_programs(0), pos_ref, pos_next_ref, y_hbm, buf_ref, sem,
                             TOP_K * tc)
    wt = wt_ref[...]
    ns = x_ref.shape[1] // LANES
    for s in range(ns):
        sl = slice(s * LANES, (s + 1) * LANES)
        out = jnp.zeros((tc, LANES), F32)
        for k in range(TOP_K):
            out = out + wt[:, k:k + 1] * buf_ref[slot, pl.ds(k * tc * ns + s, tc, stride=ns), :]
        o_ref[:, sl] = x_ref[:, sl] + gt_ref[:, sl] * out


def _combine(x2, gt, wts, pos, y_sorted, seq):
    t, d = x2.shape
    tc = _tile(seq, 128)
    per_b = seq // tc
    steps = t // tc
    ns = d // LANES
    y_sorted = y_sorted.reshape(-1, ns, LANES)
    pos_tiles = jnp.swapaxes(pos.reshape(steps, tc, TOP_K), 1, 2).reshape(steps, 1, TOP_K * tc)
    idx_spec = lambda f: pl.BlockSpec((None, 1, TOP_K * tc), f, memory_space=pltpu.SMEM)
    return pl.pallas_call(
        functools.partial(_combine_kernel, tc=tc),
        grid=(steps,),
        in_specs=[
            idx_spec(lambda i: (i, 0, 0)),
            idx_spec(lambda i: (jnp.minimum(i + 1, steps - 1), 0, 0)),
            pl.BlockSpec((tc, d), lambda i: (i, 0)),
            pl.BlockSpec((None, 1, d), lambda i: (i // per_b, 0, 0)),
            pl.BlockSpec((tc, LANES), lambda i: (i, 0)),
            pl.BlockSpec(memory_space=pl.ANY),
        ],
        out_specs=pl.BlockSpec((tc, d), lambda i: (i, 0)),
        out_shape=jax.ShapeDtypeStruct((t, d), F32),
        scratch_shapes=[pltpu.VMEM((2, TOP_K * tc * ns, LANES), F32), pltpu.SemaphoreType.DMA((2,))],
        compiler_params=_params("arbitrary"),
        name="moe_combine",
    )(pos_tiles, pos_tiles, x2, gt, wts, y_sorted)


def _routing_plan(top_idx, n_exp, tm):
    t = top_idx.shape[0]
    pairs = t * TOP_K
    e_flat = top_idx.reshape(pairs)
    onehot = (e_flat[:, None] == jnp.arange(n_exp, dtype=jnp.int32)[None, :]).astype(jnp.int32)
    csum = jnp.cumsum(onehot, axis=0)
    rank = jnp.sum(csum * onehot, axis=1) - 1
    counts = csum[-1]
    tiles_e = (counts + tm - 1) // tm
    tile_end = jnp.cumsum(tiles_e)
    tile_start = tile_end - tiles_e
    pos = tile_start[e_flat] * tm + rank
    n_tiles = pairs // tm + n_exp
    row_token = jnp.zeros((n_tiles * tm,), jnp.int32).at[pos].set(jnp.arange(pairs, dtype=jnp.int32) // TOP_K)
    tile_ids = jnp.arange(n_tiles, dtype=jnp.int32)
    owner = jnp.sum((tile_end[None, :] <= tile_ids[:, None]).astype(jnp.int32), axis=1)
    tile_expert = jnp.minimum(owner, n_exp - 1)
    tile_valid = (tile_ids < tile_end[-1]).astype(jnp.int32)
    last_valid = jnp.maximum(tile_end[-1] - 1, 0)
    tile_expert = jnp.where(tile_valid > 0, tile_expert, tile_expert[last_valid])
    return pos.astype(jnp.int32), row_token, tile_expert, tile_valid


def _rope_tables(positions, dim):
    inv_freq = 1.0 / (ROPE_THETA ** (jnp.arange(0, dim, 2, dtype=F32) / dim))
    ang = positions.astype(F32).reshape(-1, 1) * inv_freq
    return jnp.cos(ang), jnp.sin(ang)


def kernel(x, c, positions, w_mod, b_mod, g_norm1, g_norm2, w_in, g_kv, w_kv_up, w_proj_a, w_proj_b, w_out,
           w_router, b_router, w_moe1, b_moe1, w_moe2, b_moe2, g_final):
    bsz, seq, d = x.shape
    depth = w_mod.shape[0]
    n_exp = w_router.shape[-1]
    t = bsz * seq
    n_sel = min(TOPK_MAX, seq // 4)
    ck = _tile(seq, 512)
    tm_moe = 512 if (t * TOP_K) % 512 == 0 else 64

    cos, sin = _rope_tables(positions, HEAD_DIM)
    cos_a = jnp.concatenate([cos, cos], axis=1)
    sin_a = jnp.concatenate([-sin, sin], axis=1)
    cos_i, sin_i = _rope_tables(positions, IDX_DIM)
    cos_q = jnp.tile(cos_i, (1, LANES // (IDX_DIM // 2)))
    sin_q = jnp.tile(jnp.concatenate([-sin_i, sin_i], axis=1), (1, LANES // IDX_DIM))
    lane_is_k = (jnp.arange(LANES) < IDX_DIM)[None, :]
    cos_k = jnp.where(lane_is_k, cos_q, 1.0)
    sin_k = jnp.where(lane_is_k, sin_q, 0.0)

    mod = _modulation(c, w_mod, b_mod)
    x2 = x.reshape(t, d)

    o1, o2, o3, o4, o5, o6 = (A_Q_WIDTH, A_Q_WIDTH + KV_RANK, A_Q_WIDTH + KV_RANK + IDX_Q_WIDTH,
                              A_Q_WIDTH + KV_RANK + IDX_Q_WIDTH + IDX_DIM,
                              A_Q_WIDTH + KV_RANK + IDX_Q_WIDTH + IDX_DIM + IDX_HEADS,
                              A_Q_WIDTH + KV_RANK + IDX_Q_WIDTH + IDX_DIM + IDX_HEADS + B_QKV_WIDTH)

    for l in range(depth):
        mvec = lambda n: mod[l, :, n * d:(n + 1) * d].reshape(bsz, 1, d)
        sh1, sc1, gt1, sh2, sc2, gt2 = (mvec(n) for n in range(N_MOD))

        wl = w_in[l]
        w_qa = wl[:, :o1].astype(BF16)
        w_ckv = wl[:, o1:o2].astype(BF16)
        w_qi = wl[:, o2:o3].astype(BF16)
        w_kw = jnp.pad(wl[:, o3:o5], ((0, 0), (0, LANES - (o5 - o3)))).astype(BF16)
        w_qkvb = wl[:, o5:o6].astype(BF16)
        w_gate = wl[:, o6:].astype(BF16)
        wkv = w_kv_up[l].reshape(KV_RANK, A_HEADS, 2, HEAD_DIM)
        w_k = wkv[:, :, 0].reshape(KV_RANK, A_Q_WIDTH).astype(BF16)
        w_v = wkv[:, :, 1].reshape(KV_RANK, A_Q_WIDTH).astype(BF16)

        h = _norm(x2, g_norm1[l], seq, sh1, sc1)
        qa = _mm(h, w_qa, epi="rope64", tables=(cos_a, sin_a))
        ka, va = _kv_proj(h, w_ckv, g_kv[l], w_k, w_v, cos_a, sin_a)
        qi = _mm(h, w_qi, epi="rope32", tables=(cos_q, sin_q))
        kw = _mm(h, w_kw, epi="rope32", tables=(cos_k, sin_k), out_dtype=F32)
        ki = kw[:, :IDX_DIM].astype(BF16).reshape(bsz, seq, IDX_DIM)
        w_t = jnp.swapaxes(kw[:, IDX_DIM:IDX_DIM + IDX_HEADS].reshape(bsz, seq, IDX_HEADS), 1, 2)
        n_rope_tiles = (2 * B_GROUPS * B_OUT_WIDTH) // _tile(B_QKV_WIDTH, 512)
        qkvb = _mm(h, w_qkvb, epi="rope64", tables=(cos_a, sin_a), rope_cols=n_rope_tiles)
        gates = _mm(h, w_gate, epi="sigmoid")

        bias = _index_select(qi.reshape(bsz, seq, IDX_Q_WIDTH), ki, w_t, n_sel, ck)
        o_a = _dsa_attention(qa.reshape(bsz, seq, A_Q_WIDTH), ka.reshape(bsz, seq, A_Q_WIDTH),
                             va.reshape(bsz, seq, A_Q_WIDTH), bias, ck)
        qkvb3 = qkvb.reshape(bsz, seq, B_QKV_WIDTH)
        o_b, lse_b = zip(*[_dilated_group(qkvb3, g, win, dil) for g, (win, dil) in enumerate(B_PATTERNS)])
        merged = _merge(o_a.reshape(t, A_Q_WIDTH), o_b, lse_b, gates,
                        w_proj_a[l].astype(BF16), w_proj_b[l].astype(BF16))
        x2 = _mm(merged, w_out[l].astype(BF16), epi="resid", resid=(x2, gt1), seq=seq, out_dtype=F32,
                 tm_pref=512)

        h2 = _norm(x2, g_norm2[l], seq, sh2, sc2, out_dtype=F32, slab=True)
        w_r = jnp.pad(w_router[l], ((0, 0), (0, LANES - n_exp))).astype(BF16)
        b_r = jnp.pad(b_router[l], (0, LANES - n_exp)).reshape(1, LANES)
        top_idx, top_w = _router(h2, w_r, b_r, n_exp)
        pos, row_token, tile_expert, tile_valid = _routing_plan(top_idx[:, :TOP_K], n_exp, tm_moe)
        f2 = w_moe1.shape[-1]
        b1p = jnp.swapaxes(b_moe1[l].reshape(n_exp, f2 // (2 * LANES), LANES, 2), 2, 3).reshape(n_exp, 1, f2)
        y_sorted = _expert_ffn(h2, row_token, tile_expert, tile_valid, _deinterleave(w_moe1[l]), b1p,
                               w_moe2[l].astype(BF16), b_moe2[l].reshape(n_exp, 1, d), tm_moe)
        x2 = _combine(x2, gt2, top_w, pos, y_sorted, seq)

    return _norm(x2, g_final, seq, out_dtype=x.dtype).reshape(bsz, seq, d)
```

```python
import functools

import jax
import jax.numpy as jnp
from jax import lax
from jax.experimental import pallas as pl
from jax.experimental.pallas import tpu as pltpu

HEAD_DIM = 128
ROPE_THETA = 10000.0
RMS_EPS = 1e-5
A_HEADS = 8
KV_RANK = 512
IDX_HEADS = 16
IDX_DIM = 64
TOPK_MAX = 256
B_PATTERNS = ((128, 1), (512, 4), (2048, 16))
B_GROUPS = 3
B_HEADS_PER_GROUP = 4
N_MOD = 6
TOP_K = 4
SWIGLU_LIMIT = 7.0
SWIGLU_ALPHA = 1.702

LANES = 128
NEG = -1e30
INT_MIN = -(2 ** 31)
VMEM_LIMIT = 48 * 1024 * 1024
GATHER_UNROLL = 8
COMBINE_ROWS = 32

A_Q_WIDTH = A_HEADS * HEAD_DIM
IDX_Q_WIDTH = IDX_HEADS * IDX_DIM
B_OUT_WIDTH = B_HEADS_PER_GROUP * HEAD_DIM
B_QKV_WIDTH = 3 * B_GROUPS * B_OUT_WIDTH

F32 = jnp.float32
BF16 = jnp.bfloat16
NT_DIMS = (((1,), (1,)), ((), ()))


def _tile(n, pref):
    return pref if n % pref == 0 else n


def _params(*sem):
    return pltpu.CompilerParams(dimension_semantics=sem, vmem_limit_bytes=VMEM_LIMIT)


def _dot(a, b):
    return jnp.dot(a, b, preferred_element_type=F32)


def _dot_nt(a, b):
    return lax.dot_general(a, b, NT_DIMS, preferred_element_type=F32)


def _mod_kernel(c_ref, w_ref, b_ref, o_ref):
    c = c_ref[...]
    c_act = c * jax.nn.sigmoid(c)
    o_ref[...] = _dot(c_act, w_ref[...]) + b_ref[...]


def _modulation(c, w_mod, b_mod):
    depth, d, n = w_mod.shape
    bsz = c.shape[0]
    tn = _tile(n, 1024)
    return pl.pallas_call(
        _mod_kernel,
        grid=(depth, n // tn),
        in_specs=[
            pl.BlockSpec((bsz, d), lambda l, j: (0, 0)),
            pl.BlockSpec((None, d, tn), lambda l, j: (l, 0, j)),
            pl.BlockSpec((None, 1, tn), lambda l, j: (l, 0, j)),
        ],
        out_specs=pl.BlockSpec((None, bsz, tn), lambda l, j: (l, 0, j)),
        out_shape=jax.ShapeDtypeStruct((depth, bsz, n), F32),
        compiler_params=_params("parallel", "parallel"),
        name="adaln_mod",
    )(c, w_mod, b_mod.reshape(depth, 1, n))


def _store_slabs(ref, y):
    rows, ns = y.shape[0], y.shape[1] // LANES
    for s in range(ns):
        ref[pl.ds(s, rows, stride=ns), :] = y[:, s * LANES:(s + 1) * LANES].astype(ref.dtype)


def _load_slabs(ref, rows, dtype):
    ns = ref.shape[0] // rows
    return jnp.concatenate([ref[pl.ds(s, rows, stride=ns), :].astype(dtype) for s in range(ns)], axis=1)


def _norm_kernel(x_ref, g_ref, *rest, modulate, slab):
    x = x_ref[...]
    y = x * lax.rsqrt(jnp.mean(x * x, axis=-1, keepdims=True) + RMS_EPS) * g_ref[...]
    if modulate:
        sh_ref, sc_ref, o_ref = rest
        y = y * (1.0 + sc_ref[...]) + sh_ref[...]
    else:
        (o_ref,) = rest
    if slab:
        _store_slabs(o_ref, y)
    else:
        o_ref[...] = y.astype(o_ref.dtype)


def _norm(x2, g, seq, shift=None, scale=None, out_dtype=BF16, slab=False):
    t, d = x2.shape
    tm = _tile(seq, 512)
    per_b = seq // tm
    in_specs = [pl.BlockSpec((tm, d), lambda i: (i, 0)), pl.BlockSpec((1, d), lambda i: (0, 0))]
    args = [x2, g.reshape(1, d)]
    if shift is not None:
        vec = pl.BlockSpec((None, 1, d), lambda i: (i // per_b, 0, 0))
        in_specs += [vec, vec]
        args += [shift, scale]
    if slab:
        ns = d // LANES
        out_spec = pl.BlockSpec((tm * ns, LANES), lambda i: (i, 0))
        out_shape = jax.ShapeDtypeStruct((t * ns, LANES), out_dtype)
    else:
        out_spec = pl.BlockSpec((tm, d), lambda i: (i, 0))
        out_shape = jax.ShapeDtypeStruct((t, d), out_dtype)
    return pl.pallas_call(
        functools.partial(_norm_kernel, modulate=shift is not None, slab=slab),
        grid=(t // tm,),
        in_specs=in_specs,
        out_specs=out_spec,
        out_shape=out_shape,
        compiler_params=_params("parallel"),
        name="rmsnorm_mod",
    )(*args)


def _rope_cols(x, cos, sin, half):
    outs = []
    for c in range(x.shape[1] // LANES):
        xc = x[:, c * LANES:(c + 1) * LANES]
        if half == LANES // 2:
            partner = pltpu.roll(xc, LANES // 2, 1)
        else:
            lane = lax.broadcasted_iota(jnp.int32, xc.shape, 1)
            low = (lane & (2 * half - 1)) < half
            partner = jnp.where(low, pltpu.roll(xc, LANES - half, 1), pltpu.roll(xc, half, 1))
        outs.append(xc * cos + partner * sin)
    return outs[0] if len(outs) == 1 else jnp.concatenate(outs, axis=1)


def _mm_kernel(h_ref, w_ref, *rest, epi, rope_cols):
    acc = _dot(h_ref[...], w_ref[...])
    if epi in ("rope64", "rope32"):
        cos_ref, sin_ref, o_ref = rest
        half = 64 if epi == "rope64" else 32
        if rope_cols is None:
            acc = _rope_cols(acc, cos_ref[...], sin_ref[...], half)
        else:
            j = pl.program_id(1)
            roped = _rope_cols(acc, cos_ref[...], sin_ref[...], half)
            acc = jnp.where(j < rope_cols, roped, acc)
    elif epi == "sigmoid":
        (o_ref,) = rest
        acc = jax.nn.sigmoid(acc)
    elif epi == "resid":
        x_ref, gt_ref, o_ref = rest
        acc = x_ref[...] + gt_ref[...] * acc
    else:
        (o_ref,) = rest
    o_ref[...] = acc.astype(o_ref.dtype)


def _mm(h, w, *, epi="none", tables=None, resid=None, seq=None, out_dtype=BF16, rope_cols=None,
        tm_pref=1024, tn_pref=512):
    t, k = h.shape
    n = w.shape[1]
    tm = _tile(t if seq is None else seq, tm_pref)
    tn = _tile(n, tn_pref)
    in_specs = [pl.BlockSpec((tm, k), lambda i, j: (i, 0)), pl.BlockSpec((k, tn), lambda i, j: (0, j))]
    args = [h, w]
    if tables is not None:
        spec = pl.BlockSpec((tm, LANES), lambda i, j: (i, 0))
        in_specs += [spec, spec]
        args += list(tables)
    if resid is not None:
        x2, gt = resid
        per_b = seq // tm
        in_specs += [pl.BlockSpec((tm, tn), lambda i, j: (i, j)),
                     pl.BlockSpec((None, 1, tn), lambda i, j: (i // per_b, 0, j))]
        args += [x2, gt]
    return pl.pallas_call(
        functools.partial(_mm_kernel, epi=epi, rope_cols=rope_cols),
        grid=(t // tm, n // tn),
        in_specs=in_specs,
        out_specs=pl.BlockSpec((tm, tn), lambda i, j: (i, j)),
        out_shape=jax.ShapeDtypeStruct((t, n), out_dtype),
        compiler_params=_params("parallel", "parallel"),
        name="proj_" + epi,
    )(*args)


def _kv_kernel(h_ref, wc_ref, g_ref, wk_ref, wv_ref, cos_ref, sin_ref, k_ref, v_ref):
    ckv = _dot(h_ref[...], wc_ref[...])
    y = ckv * lax.rsqrt(jnp.mean(ckv * ckv, axis=-1, keepdims=True) + RMS_EPS) * g_ref[...]
    yb = y.astype(BF16)
    k = _dot(yb, wk_ref[...])
    k_ref[...] = _rope_cols(k, cos_ref[...], sin_ref[...], HEAD_DIM // 2).astype(k_ref.dtype)
    v_ref[...] = _dot(yb, wv_ref[...]).astype(v_ref.dtype)


def _kv_proj(h, w_ckv, g_kv, w_k, w_v, cos, sin):
    t, d = h.shape
    tm = _tile(t, 512)
    full = lambda a: pl.BlockSpec(a.shape, lambda i: (0, 0))
    row = lambda width: pl.BlockSpec((tm, width), lambda i: (i, 0))
    g2 = g_kv.reshape(1, KV_RANK)
    return pl.pallas_call(
        _kv_kernel,
        grid=(t // tm,),
        in_specs=[row(d), full(w_ckv), full(g2), full(w_k), full(w_v), row(LANES), row(LANES)],
        out_specs=[row(A_Q_WIDTH), row(A_Q_WIDTH)],
        out_shape=[jax.ShapeDtypeStruct((t, A_Q_WIDTH), BF16)] * 2,
        compiler_params=_params("parallel"),
        name="kv_latent",
    )(h, w_ckv, g2, w_k, w_v, cos, sin)


def _idx_kernel(qi_ref, ki_ref, w_ref, o_ref, key_ref, *, tq, ck, n_sel, nkc, scale):
    qb = pl.program_id(1)
    nchunk = (qb * tq + tq + ck - 1) // ck
    q_idx = qb * tq + lax.broadcasted_iota(jnp.int32, (ck, tq), 1)
    s_loc = lax.broadcasted_iota(jnp.int32, (ck, tq), 0)

    def score_chunk(c, carry):
        off = pl.multiple_of(c * ck, ck)
        k = ki_ref[pl.ds(off, ck), :]
        acc = jnp.zeros((ck, tq), F32)
        for h in range(IDX_HEADS):
            d = _dot_nt(k, qi_ref[:, h * IDX_DIM:(h + 1) * IDX_DIM])
            acc = acc + jnp.maximum(d, 0.0) * w_ref[h:h + 1, :]
        bits = pltpu.bitcast(acc * scale, jnp.int32)
        key = jnp.where(bits < 0, bits ^ jnp.int32(0x7FFFFFFF), bits)
        key_ref[pl.ds(off, ck), :] = jnp.where(off + s_loc <= q_idx, key, jnp.int32(INT_MIN))
        return carry

    lax.fori_loop(0, nchunk, score_chunk, 0)

    def bit_step(i, prefix):
        cand_u = prefix | lax.shift_left(jnp.int32(1), 31 - i)
        cand = cand_u ^ jnp.int32(INT_MIN)

        def count_chunk(c, cnt):
            off = pl.multiple_of(c * ck, ck)
            ge = key_ref[pl.ds(off, ck), :] >= cand
            return cnt + jnp.sum(ge.astype(jnp.int32), axis=0, keepdims=True)

        cnt = lax.fori_loop(0, nchunk, count_chunk, jnp.zeros((1, tq), jnp.int32))
        return jnp.where(cnt >= n_sel, cand_u, prefix)

    prefix = lax.fori_loop(0, 32, bit_step, jnp.zeros((1, tq), jnp.int32))
    thr = jnp.maximum(prefix ^ jnp.int32(INT_MIN), jnp.int32(INT_MIN + 1))

    def write_chunk(c, carry):
        off = pl.multiple_of(c * ck, ck)
        sel = key_ref[pl.ds(off, ck), :] >= thr
        o_ref[c] = jnp.where(sel, 0.0, NEG).astype(F32).T.astype(o_ref.dtype)
        return carry

    lax.fori_loop(0, nchunk, write_chunk, 0)

    def fill_chunk(c, carry):
        o_ref[c] = jnp.full((tq, ck), NEG, o_ref.dtype)
        return carry

    lax.fori_loop(nchunk, nkc, fill_chunk, 0)


def _index_select(qi, ki, w_t, n_sel, ck):
    bsz, seq, _ = qi.shape
    tq = _tile(seq, LANES)
    nkc = seq // ck
    scale = (IDX_DIM ** -0.5) * (IDX_HEADS ** -0.5)
    return pl.pallas_call(
        functools.partial(_idx_kernel, tq=tq, ck=ck, n_sel=n_sel, nkc=nkc, scale=scale),
        grid=(bsz, seq // tq),
        in_specs=[
            pl.BlockSpec((None, tq, IDX_Q_WIDTH), lambda b, q: (b, q, 0)),
            pl.BlockSpec((None, seq, IDX_DIM), lambda b, q: (b, 0, 0)),
            pl.BlockSpec((None, IDX_HEADS, tq), lambda b, q: (b, 0, q)),
        ],
        out_specs=pl.BlockSpec((None, nkc, tq, ck), lambda b, q: (b, 0, q, 0)),
        out_shape=jax.ShapeDtypeStruct((bsz, nkc, seq, ck), BF16),
        scratch_shapes=[pltpu.VMEM((seq, tq), jnp.int32)],
        compiler_params=_params("parallel", "parallel"),
        name="indexer_select",
    )(qi, ki, w_t)


def _dsa_kernel(q_ref, k_ref, v_ref, b_ref, o_ref, m_ref, l_ref, acc_ref, *, tq, tk, scale):
    qb = pl.program_id(1)
    kb = pl.program_id(2)
    last = ((qb + 1) * tq - 1) // tk

    @pl.when(kb == 0)
    def _():
        m_ref[...] = jnp.full(m_ref.shape, NEG, F32)
        l_ref[...] = jnp.zeros(l_ref.shape, F32)
        acc_ref[...] = jnp.zeros(acc_ref.shape, F32)

    @pl.when(kb <= last)
    def _():
        bias = b_ref[...].astype(F32)
        for h in range(A_HEADS):
            sl = slice(h * HEAD_DIM, (h + 1) * HEAD_DIM)
            s = _dot_nt(q_ref[:, sl], k_ref[:, sl]) * scale + bias
            m_prev = m_ref[h]
            m_new = jnp.maximum(m_prev, jnp.max(s, axis=1, keepdims=True))
            alpha = jnp.exp(m_prev - m_new)
            p = jnp.exp(s - m_new[:, :1])
            l_ref[h] = alpha * l_ref[h] + jnp.sum(p, axis=1, keepdims=True)
            m_ref[h] = m_new
            acc_ref[:, sl] = acc_ref[:, sl] * alpha + _dot(p.astype(BF16), v_ref[:, sl])

    @pl.when(kb == last)
    def _():
        for h in range(A_HEADS):
            sl = slice(h * HEAD_DIM, (h + 1) * HEAD_DIM)
            o_ref[:, sl] = (acc_ref[:, sl] / l_ref[h]).astype(o_ref.dtype)


def _dsa_attention(q, k, v, bias, tk):
    bsz, seq, width = q.shape
    tq = _tile(seq, 256)
    last = lambda qb: ((qb + 1) * tq - 1) // tk
    kv_spec = pl.BlockSpec((None, tk, width), lambda b, qb, kb: (b, jnp.minimum(kb, last(qb)), 0))
    return pl.pallas_call(
        functools.partial(_dsa_kernel, tq=tq, tk=tk, scale=HEAD_DIM ** -0.5),
        grid=(bsz, seq // tq, seq // tk),
        in_specs=[
            pl.BlockSpec((None, tq, width), lambda b, qb, kb: (b, qb, 0)),
            kv_spec,
            kv_spec,
            pl.BlockSpec((None, None, tq, tk), lambda b, qb, kb: (b, jnp.minimum(kb, last(qb)), qb, 0)),
        ],
        out_specs=pl.BlockSpec((None, tq, width), lambda b, qb, kb: (b, qb, 0)),
        out_shape=jax.ShapeDtypeStruct((bsz, seq, width), BF16),
        scratch_shapes=[
            pltpu.VMEM((A_HEADS, tq, LANES), F32),
            pltpu.VMEM((A_HEADS, tq, LANES), F32),
            pltpu.VMEM((tq, width), F32),
        ],
        compiler_params=_params("parallel", "parallel", "arbitrary"),
        name="dsa_flash",
    )(q, k, v, bias)


def _dil_kernel(q_ref, kc_ref, kp_ref, vc_ref, vp_ref, o_ref, lse_ref, *, w, scale):
    i = pl.program_id(2)
    row = lax.broadcasted_iota(jnp.int32, (w, w), 0)
    col = lax.broadcasted_iota(jnp.int32, (w, w), 1)
    prev_ok = (col >= row) & (i > 0)
    cur_ok = col <= row
    for hh in range(B_HEADS_PER_GROUP):
        sl = slice(hh * HEAD_DIM, (hh + 1) * HEAD_DIM)
        q = q_ref[:, sl]
        sp = jnp.where(prev_ok, _dot_nt(q, kp_ref[:, sl]) * scale, NEG)
        sc = jnp.where(cur_ok, _dot_nt(q, kc_ref[:, sl]) * scale, NEG)
        m = jnp.maximum(jnp.max(sp, axis=1, keepdims=True), jnp.max(sc, axis=1, keepdims=True))
        pp = jnp.exp(sp - m)
        pc = jnp.exp(sc - m)
        l = jnp.sum(pp, axis=1, keepdims=True) + jnp.sum(pc, axis=1, keepdims=True)
        o = _dot(pp.astype(BF16), vp_ref[:, sl]) + _dot(pc.astype(BF16), vc_ref[:, sl])
        o_ref[:, sl] = o / l
        lse_ref[:, sl] = jnp.broadcast_to(m + jnp.log(l), (w, HEAD_DIM))


def _dilated_group(qkv, window, dil):
    bsz, seq, width = qkv.shape
    w = window // dil
    sub = seq // dil
    assert sub % w == 0
    nblk = width // B_OUT_WIDTH
    view = qkv.reshape(bsz, sub, dil * width)
    col = lambda which: (lambda b, r, i: (b, i, r * nblk + which))
    colp = lambda which: (lambda b, r, i: (b, jnp.maximum(i - 1, 0), r * nblk + which))
    blk = lambda f: pl.BlockSpec((None, w, B_OUT_WIDTH), f)
    out_spec = pl.BlockSpec((None, w, B_OUT_WIDTH), lambda b, r, i: (b, i, r))
    o, lse = pl.pallas_call(
        functools.partial(_dil_kernel, w=w, scale=HEAD_DIM ** -0.5),
        grid=(bsz, dil, sub // w),
        in_specs=[blk(col(0)), blk(col(1)), blk(colp(1)), blk(col(2)), blk(colp(2))],
        out_specs=[out_spec, out_spec],
        out_shape=[jax.ShapeDtypeStruct((bsz, sub, dil * B_OUT_WIDTH), F32)] * 2,
        compiler_params=_params("parallel", "parallel", "parallel"),
        name=f"dilated_d{dil}",
    )(view, view, view, view, view)
    return o.reshape(bsz * seq, B_OUT_WIDTH), lse.reshape(bsz * seq, B_OUT_WIDTH)


def _merge_kernel(oa_ref, o0, o1, o2, l0, l1, l2, ga_ref, gb_ref, wa_ref, wb_ref, out_ref):
    m = jnp.maximum(jnp.maximum(l0[...], l1[...]), l2[...])
    e0 = jnp.exp(l0[...] - m)
    e1 = jnp.exp(l1[...] - m)
    e2 = jnp.exp(l2[...] - m)
    ob = (e0 * o0[...] + e1 * o1[...] + e2 * o2[...]) / (e0 + e1 + e2)
    pa = _dot(oa_ref[...], wa_ref[...])
    pb = _dot(ob.astype(BF16), wb_ref[...])
    out_ref[...] = (ga_ref[...].astype(F32) * pa + gb_ref[...].astype(F32) * pb).astype(out_ref.dtype)


def _merge(o_a, o_b, lse_b, gates, w_pa, w_pb):
    t = o_a.shape[0]
    d = w_pa.shape[1]
    tm = _tile(t, 256)
    row = lambda width, cb=0: pl.BlockSpec((tm, width), lambda i: (i, cb))
    full = lambda a: pl.BlockSpec(a.shape, lambda i: (0, 0))
    return pl.pallas_call(
        _merge_kernel,
        grid=(t // tm,),
        in_specs=[row(A_Q_WIDTH)] + [row(B_OUT_WIDTH)] * 6 + [row(d, 0), row(d, 1), full(w_pa), full(w_pb)],
        out_specs=row(d),
        out_shape=jax.ShapeDtypeStruct((t, d), BF16),
        compiler_params=_params("parallel"),
        name="branch_merge",
    )(o_a, *o_b, *lse_b, gates, gates, w_pa, w_pb)


def _router_kernel(h_ref, w_ref, b_ref, idx_ref, wt_ref, *, n_exp):
    lg = _dot(_load_slabs(h_ref, idx_ref.shape[0], BF16), w_ref[...]) + b_ref[...]
    lane = lax.broadcasted_iota(jnp.int32, lg.shape, 1)
    lg = jnp.where(lane < n_exp, lg, -jnp.inf)
    vals, idxs = [], []
    for _ in range(TOP_K):
        m = jnp.max(lg, axis=1, keepdims=True)
        idx = jnp.min(jnp.where(lg == m, lane, LANES), axis=1, keepdims=True)
        vals.append(m)
        idxs.append(idx)
        lg = jnp.where(lane == idx, -jnp.inf, lg)
    es = [jnp.exp(v - vals[0]) for v in vals]
    tot = es[0] + es[1] + es[2] + es[3]
    idx_out = jnp.zeros(lg.shape, jnp.int32)
    wt_out = jnp.zeros(lg.shape, F32)
    for k in range(TOP_K):
        idx_out = jnp.where(lane == k, idxs[k], idx_out)
        wt_out = jnp.where(lane == k, es[k] / tot, wt_out)
    idx_ref[...] = idx_out
    wt_ref[...] = wt_out


def _router(h, w_r, b_r, n_exp):
    ns = w_r.shape[0] // LANES
    t = h.shape[0] // ns
    tm = _tile(t, 512)
    row = lambda width: pl.BlockSpec((tm, width), lambda i: (i, 0))
    full = lambda a: pl.BlockSpec(a.shape, lambda i: (0, 0))
    return pl.pallas_call(
        functools.partial(_router_kernel, n_exp=n_exp),
        grid=(t // tm,),
        in_specs=[pl.BlockSpec((tm * ns, LANES), lambda i: (i, 0)), full(w_r), full(b_r)],
        out_specs=[row(LANES), row(LANES)],
        out_shape=[jax.ShapeDtypeStruct((t, LANES), jnp.int32), jax.ShapeDtypeStruct((t, LANES), F32)],
        compiler_params=_params("parallel"),
        name="router_top4",
    )(h, w_r, b_r)


def _row_copy(src_hbm, dst_ref, sem, src_row, dst_row):
    ns = src_hbm.shape[1]
    dst = dst_ref.at[pl.ds(pl.multiple_of(dst_row * ns, ns), ns)]
    return pltpu.make_async_copy(src_hbm.at[src_row], dst, sem)


def _gather_start(idx_ref, src_hbm, dst_ref, sem, n, priority):
    def body(r, carry):
        _row_copy(src_hbm, dst_ref, sem, idx_ref[0, r], r).start(priority=priority)
        return carry

    lax.fori_loop(0, n, body, 0, unroll=GATHER_UNROLL)


def _gather_wait(src_hbm, dst_ref, sem, n):
    def body(r, carry):
        _row_copy(src_hbm, dst_ref, sem, 0, r).wait()
        return carry

    lax.fori_loop(0, n, body, 0, unroll=GATHER_UNROLL)


def _gather_pipelined(i, n_steps, idx_ref, idx_next_ref, src_hbm, buf_ref, sem, n, priority=0):
    slot = i % 2

    @pl.when(i == 0)
    def _():
        _gather_start(idx_ref, src_hbm, buf_ref.at[0], sem.at[0], n, priority)

    @pl.when(i + 1 < n_steps)
    def _():
        _gather_start(idx_next_ref, src_hbm, buf_ref.at[1 - slot], sem.at[1 - slot], n, priority)

    _gather_wait(src_hbm, buf_ref.at[slot], sem.at[slot], n)
    return slot


def _deint_kernel(w_ref, p_ref, o_ref):
    p = p_ref[...]
    for g in range(w_ref.shape[1] // (2 * LANES)):
        sl = slice(g * 2 * LANES, (g + 1) * 2 * LANES)
        o_ref[:, sl] = _dot(w_ref[:, sl].astype(BF16), p).astype(o_ref.dtype)


def _deinterleave(w, layer):
    _, n_exp, d, n = w.shape
    tn = _tile(n, 512)
    k = lax.broadcasted_iota(jnp.int32, (2 * LANES, 2 * LANES), 0)
    col = lax.broadcasted_iota(jnp.int32, (2 * LANES, 2 * LANES), 1)
    perm = (k == 2 * (col % LANES) + col // LANES).astype(BF16)
    return pl.pallas_call(
        _deint_kernel,
        grid=(n_exp, n // tn),
        in_specs=[pl.BlockSpec((None, None, d, tn), lambda e, j: (layer, e, 0, j)),
                  pl.BlockSpec(perm.shape, lambda e, j: (0, 0))],
        out_specs=pl.BlockSpec((None, d, tn), lambda e, j: (e, 0, j)),
        out_shape=jax.ShapeDtypeStruct((n_exp, d, n), BF16),
        compiler_params=_params("parallel", "parallel"),
        name="moe_w1_layout",
    )(w, perm)


def _cast_kernel(w_ref, o_ref):
    o_ref[...] = w_ref[...].astype(o_ref.dtype)


def _cast_layer_bf16(w, layer):
    _, n_exp, f, d = w.shape
    tf = _tile(f, 512)
    return pl.pallas_call(
        _cast_kernel,
        grid=(n_exp, f // tf),
        in_specs=[pl.BlockSpec((None, None, tf, d), lambda e, j: (layer, e, j, 0))],
        out_specs=pl.BlockSpec((None, tf, d), lambda e, j: (e, j, 0)),
        out_shape=jax.ShapeDtypeStruct((n_exp, f, d), BF16),
        compiler_params=_params("parallel", "parallel"),
        name="moe_w2_cast",
    )(w)


def _expert_kernel(te_ref, tv_ref, tok_ref, tok_next_ref, h_hbm, w1_ref, b1_ref, w2_ref, b2_ref, y_ref,
                   xg_ref, xb_ref, acc_ref, sem, *, tm):
    i = pl.program_id(0)
    j = pl.program_id(1)
    nj = pl.num_programs(1)
    valid = tv_ref[i] > 0

    @pl.when(j == 0)
    def _():
        slot = _gather_pipelined(i, pl.num_programs(0), tok_ref, tok_next_ref, h_hbm, xg_ref, sem, tm, priority=1)
        xb_ref[...] = _load_slabs(xg_ref.at[slot], tm, BF16)
        acc_ref[...] = jnp.zeros(acc_ref.shape, F32)

    @pl.when(valid)
    def _():
        a = _dot(xb_ref[...], w1_ref[...]) + b1_ref[...]
        acts = []
        for g in range(a.shape[1] // (2 * LANES)):
            glu = jnp.minimum(a[:, 2 * g * LANES:(2 * g + 1) * LANES], SWIGLU_LIMIT)
            lin = jnp.clip(a[:, (2 * g + 1) * LANES:(2 * g + 2) * LANES], -SWIGLU_LIMIT, SWIGLU_LIMIT)
            acts.append((glu * jax.nn.sigmoid(SWIGLU_ALPHA * glu) * (lin + 1.0)).astype(BF16))
        act = acts[0] if len(acts) == 1 else jnp.concatenate(acts, axis=1)
        acc_ref[...] += _dot(act, w2_ref[...])

    @pl.when(j == nj - 1)
    def _():
        _store_slabs(y_ref, acc_ref[...] + b2_ref[...])


def _expert_ffn(h_slabs, row_token, tile_expert, tile_valid, w1p, b1p, w2, b2, tm):
    n_rows = row_token.shape[0]
    n_tiles = n_rows // tm
    n_exp, d, f2 = w1p.shape
    ns = d // LANES
    h_slabs = h_slabs.reshape(-1, ns, LANES)
    fc = _tile(f2 // 2, 512)
    nj = (f2 // 2) // fc
    jj = lambda j, tv, i: jnp.where(tv[i] > 0, j, nj - 1)
    tok = row_token.reshape(n_tiles, 1, tm)
    grid_spec = pltpu.PrefetchScalarGridSpec(
        num_scalar_prefetch=2,
        grid=(n_tiles, nj),
        in_specs=[
            pl.BlockSpec((None, 1, tm), lambda i, j, te, tv: (i, 0, 0), memory_space=pltpu.SMEM),
            pl.BlockSpec((None, 1, tm), lambda i, j, te, tv: (jnp.minimum(i + 1, n_tiles - 1), 0, 0),
                         memory_space=pltpu.SMEM),
            pl.BlockSpec(memory_space=pl.ANY),
            pl.BlockSpec((None, d, 2 * fc), lambda i, j, te, tv: (te[i], 0, jj(j, tv, i))),
            pl.BlockSpec((None, 1, 2 * fc), lambda i, j, te, tv: (te[i], 0, jj(j, tv, i))),
            pl.BlockSpec((None, fc, d), lambda i, j, te, tv: (te[i], jj(j, tv, i), 0)),
            pl.BlockSpec((None, 1, d), lambda i, j, te, tv: (te[i], 0, 0)),
        ],
        out_specs=pl.BlockSpec((tm * ns, LANES), lambda i, j, te, tv: (i, 0)),
        scratch_shapes=[pltpu.VMEM((2, tm * ns, LANES), F32), pltpu.VMEM((tm, d), BF16),
                        pltpu.VMEM((tm, d), F32), pltpu.SemaphoreType.DMA((2,))],
    )
    return pl.pallas_call(
        functools.partial(_expert_kernel, tm=tm),
        grid_spec=grid_spec,
        out_shape=jax.ShapeDtypeStruct((n_rows * ns, LANES), F32),
        compiler_params=_params("arbitrary", "arbitrary"),
        name="moe_experts",
    )(tile_expert, tile_valid, tok, tok, h_slabs, w1p, b1p, w2, b2)


def _combine_kernel(pos_ref, pos_next_ref, x_ref, gt_ref, wt_ref, y_hbm, o_ref, buf_ref, sem, *, tc):
    slot = _gather_pipelined(pl.program_id(0), pl.num_programs(0), pos_ref, pos_next_ref, y_hbm, buf_ref, sem,
                             TOP_K * tc)
    ns = x_ref.shape[1] // LANES
    rb = min(tc, COMBINE_ROWS)
    for r0 in range(0, tc, rb):
        wk = [jnp.broadcast_to(wt_ref[r0:r0 + rb, k:k + 1], (rb, LANES)) for k in range(TOP_K)]
        for s in range(ns):
            sl = slice(s * LANES, (s + 1) * LANES)
            out = wk[0] * buf_ref[slot, pl.ds(r0 * ns + s, rb, stride=ns), :]
            for k in range(1, TOP_K):
                out = out + wk[k] * buf_ref[slot, pl.ds((k * tc + r0) * ns + s, rb, stride=ns), :]
            o_ref[r0:r0 + rb, sl] = x_ref[r0:r0 + rb, sl] + gt_ref[:, sl] * out


def _combine(x2, gt, wts, pos, y_sorted, seq):
    t, d = x2.shape
    tc = _tile(seq, 128)
    per_b = seq // tc
    steps = t // tc
    ns = d // LANES
    y_sorted = y_sorted.reshape(-1, ns, LANES)
    pos_tiles = jnp.swapaxes(pos.reshape(steps, tc, TOP_K), 1, 2).reshape(steps, 1, TOP_K * tc)
    idx_spec = lambda f: pl.BlockSpec((None, 1, TOP_K * tc), f, memory_space=pltpu.SMEM)
    return pl.pallas_call(
        functools.partial(_combine_kernel, tc=tc),
        grid=(steps,),
        in_specs=[
            idx_spec(lambda i: (i, 0, 0)),
            idx_spec(lambda i: (jnp.minimum(i + 1, steps - 1), 0, 0)),
            pl.BlockSpec((tc, d), lambda i: (i, 0)),
            pl.BlockSpec((None, 1, d), lambda i: (i // per_b, 0, 0)),
            pl.BlockSpec((tc, LANES), lambda i: (i, 0)),
            pl.BlockSpec(memory_space=pl.ANY),
        ],
        out_specs=pl.BlockSpec((tc, d), lambda i: (i, 0)),
        out_shape=jax.ShapeDtypeStruct((t, d), F32),
        scratch_shapes=[pltpu.VMEM((2, TOP_K * tc * ns, LANES), F32), pltpu.SemaphoreType.DMA((2,))],
        compiler_params=_params("arbitrary"),
        name="moe_combine",
    )(pos_tiles, pos_tiles, x2, gt, wts, y_sorted)


def _routing_plan(top_idx, n_exp, tm):
    t = top_idx.shape[0]
    pairs = t * TOP_K
    e_flat = top_idx.reshape(pairs)
    onehot = (e_flat[:, None] == jnp.arange(n_exp, dtype=jnp.int32)[None, :]).astype(jnp.int32)
    csum = jnp.cumsum(onehot, axis=0)
    rank = jnp.sum(csum * onehot, axis=1) - 1
    counts = csum[-1]
    tiles_e = (counts + tm - 1) // tm
    tile_end = jnp.cumsum(tiles_e)
    tile_start = tile_end - tiles_e
    pos = tile_start[e_flat] * tm + rank
    n_tiles = pairs // tm + n_exp
    row_token = jnp.zeros((n_tiles * tm,), jnp.int32).at[pos].set(jnp.arange(pairs, dtype=jnp.int32) // TOP_K)
    tile_ids = jnp.arange(n_tiles, dtype=jnp.int32)
    owner = jnp.sum((tile_end[None, :] <= tile_ids[:, None]).astype(jnp.int32), axis=1)
    tile_expert = jnp.minimum(owner, n_exp - 1)
    tile_valid = (tile_ids < tile_end[-1]).astype(jnp.int32)
    last_valid = jnp.maximum(tile_end[-1] - 1, 0)
    tile_expert = jnp.where(tile_valid > 0, tile_expert, tile_expert[last_valid])
    return pos.astype(jnp.int32), row_token, tile_expert, tile_valid


def _rope_tables(positions, dim):
    inv_freq = 1.0 / (ROPE_THETA ** (jnp.arange(0, dim, 2, dtype=F32) / dim))
    ang = positions.astype(F32).reshape(-1, 1) * inv_freq
    return jnp.cos(ang), jnp.sin(ang)


def kernel(x, c, positions, w_mod, b_mod, g_norm1, g_norm2, w_in, g_kv, w_kv_up, w_proj_a, w_proj_b, w_out,
           w_router, b_router, w_moe1, b_moe1, w_moe2, b_moe2, g_final):
    bsz, seq, d = x.shape
    depth = w_mod.shape[0]
    n_exp = w_router.shape[-1]
    t = bsz * seq
    n_sel = min(TOPK_MAX, seq // 4)
    ck = _tile(seq, 512)
    tm_moe = 512 if (t * TOP_K) % 512 == 0 else 64

    cos, sin = _rope_tables(positions, HEAD_DIM)
    cos_a = jnp.concatenate([cos, cos], axis=1)
    sin_a = jnp.concatenate([-sin, sin], axis=1)
    cos_i, sin_i = _rope_tables(positions, IDX_DIM)
    cos_q = jnp.tile(cos_i, (1, LANES // (IDX_DIM // 2)))
    sin_q = jnp.tile(jnp.concatenate([-sin_i, sin_i], axis=1), (1, LANES // IDX_DIM))
    lane_is_k = (jnp.arange(LANES) < IDX_DIM)[None, :]
    cos_k = jnp.where(lane_is_k, cos_q, 1.0)
    sin_k = jnp.where(lane_is_k, sin_q, 0.0)

    mod = _modulation(c, w_mod, b_mod)
    x2 = x.reshape(t, d)

    o1, o2, o3, o4, o5, o6 = (A_Q_WIDTH, A_Q_WIDTH + KV_RANK, A_Q_WIDTH + KV_RANK + IDX_Q_WIDTH,
                              A_Q_WIDTH + KV_RANK + IDX_Q_WIDTH + IDX_DIM,
                              A_Q_WIDTH + KV_RANK + IDX_Q_WIDTH + IDX_DIM + IDX_HEADS,
                              A_Q_WIDTH + KV_RANK + IDX_Q_WIDTH + IDX_DIM + IDX_HEADS + B_QKV_WIDTH)

    for l in range(depth):
        mvec = lambda n: mod[l, :, n * d:(n + 1) * d].reshape(bsz, 1, d)
        sh1, sc1, gt1, sh2, sc2, gt2 = (mvec(n) for n in range(N_MOD))

        wl = w_in[l]
        w_qa = wl[:, :o1].astype(BF16)
        w_ckv = wl[:, o1:o2].astype(BF16)
        w_qi = wl[:, o2:o3].astype(BF16)
        w_kw = jnp.pad(wl[:, o3:o5], ((0, 0), (0, LANES - (o5 - o3)))).astype(BF16)
        w_qkvb = wl[:, o5:o6].reshape(d, 3, B_GROUPS, B_OUT_WIDTH)
        w_b = [w_qkvb[:, :, g].reshape(d, 3 * B_OUT_WIDTH).astype(BF16) for g in range(B_GROUPS)]
        w_gate = wl[:, o6:].astype(BF16)
        wkv = w_kv_up[l].reshape(KV_RANK, A_HEADS, 2, HEAD_DIM)
        w_k = wkv[:, :, 0].reshape(KV_RANK, A_Q_WIDTH).astype(BF16)
        w_v = wkv[:, :, 1].reshape(KV_RANK, A_Q_WIDTH).astype(BF16)

        h = _norm(x2, g_norm1[l], seq, sh1, sc1)
        qa = _mm(h, w_qa, epi="rope64", tables=(cos_a, sin_a))
        ka, va = _kv_proj(h, w_ckv, g_kv[l], w_k, w_v, cos_a, sin_a)
        qi = _mm(h, w_qi, epi="rope32", tables=(cos_q, sin_q))
        kw = _mm(h, w_kw, epi="rope32", tables=(cos_k, sin_k), out_dtype=F32)
        ki = kw[:, :IDX_DIM].astype(BF16).reshape(bsz, seq, IDX_DIM)
        w_t = jnp.swapaxes(kw[:, IDX_DIM:IDX_DIM + IDX_HEADS].reshape(bsz, seq, IDX_HEADS), 1, 2)
        qkv_b = [_mm(h, w_b[g], epi="rope64", tables=(cos_a, sin_a), rope_cols=2, tn_pref=B_OUT_WIDTH)
                 for g in range(B_GROUPS)]
        gates = _mm(h, w_gate, epi="sigmoid")

        bias = _index_select(qi.reshape(bsz, seq, IDX_Q_WIDTH), ki, w_t, n_sel, ck)
        o_a = _dsa_attention(qa.reshape(bsz, seq, A_Q_WIDTH), ka.reshape(bsz, seq, A_Q_WIDTH),
                             va.reshape(bsz, seq, A_Q_WIDTH), bias, ck)
        o_b, lse_b = zip(*[_dilated_group(qkv_b[g].reshape(bsz, seq, 3 * B_OUT_WIDTH), win, dil)
                           for g, (win, dil) in enumerate(B_PATTERNS)])
        merged = _merge(o_a.reshape(t, A_Q_WIDTH), o_b, lse_b, gates,
                        w_proj_a[l].astype(BF16), w_proj_b[l].astype(BF16))
        x2 = _mm(merged, w_out[l].astype(BF16), epi="resid", resid=(x2, gt1), seq=seq, out_dtype=F32,
                 tm_pref=512)

        h2 = _norm(x2, g_norm2[l], seq, sh2, sc2, out_dtype=F32, slab=True)
        w_r = jnp.pad(w_router[l], ((0, 0), (0, LANES - n_exp))).astype(BF16)
        b_r = jnp.pad(b_router[l], (0, LANES - n_exp)).reshape(1, LANES)
        top_idx, top_w = _router(h2, w_r, b_r, n_exp)
        pos, row_token, tile_expert, tile_valid = _routing_plan(top_idx[:, :TOP_K], n_exp, tm_moe)
        f2 = w_moe1.shape[-1]
        b1p = jnp.swapaxes(b_moe1[l].reshape(n_exp, f2 // (2 * LANES), LANES, 2), 2, 3).reshape(n_exp, 1, f2)
        y_sorted = _expert_ffn(h2, row_token, tile_expert, tile_valid, _deinterleave(w_moe1, l), b1p,
                               _cast_layer_bf16(w_moe2, l), b_moe2[l].reshape(n_exp, 1, d), tm_moe)
        x2 = _combine(x2, gt2, top_w, pos, y_sorted, seq)

    return _norm(x2, g_final, seq, out_dtype=x.dtype).reshape(bsz, seq, d)
```

```python
import functools

import jax
import jax.numpy as jnp
from jax import lax
from jax.experimental import pallas as pl
from jax.experimental.pallas import tpu as pltpu

HEAD_DIM = 128
ROPE_THETA = 10000.0
RMS_EPS = 1e-5
A_HEADS = 8
KV_RANK = 512
IDX_HEADS = 16
IDX_DIM = 64
TOPK_MAX = 256
B_PATTERNS = ((128, 1), (512, 4), (2048, 16))
B_GROUPS = 3
B_HEADS_PER_GROUP = 4
N_MOD = 6
TOP_K = 4
SWIGLU_LIMIT = 7.0
SWIGLU_ALPHA = 1.702

LANES = 128
NEG = -1e30
INT_MIN = -(2 ** 31)
VMEM_LIMIT = 48 * 1024 * 1024
GATHER_UNROLL = 8
COMBINE_ROWS = 32
EXPERT_CHUNK_COLS = 1024

A_Q_WIDTH = A_HEADS * HEAD_DIM
IDX_Q_WIDTH = IDX_HEADS * IDX_DIM
B_OUT_WIDTH = B_HEADS_PER_GROUP * HEAD_DIM
B_QKV_WIDTH = 3 * B_GROUPS * B_OUT_WIDTH

F32 = jnp.float32
BF16 = jnp.bfloat16
NT_DIMS = (((1,), (1,)), ((), ()))


def _tile(n, pref):
    return pref if n % pref == 0 else n


def _params(*sem):
    return pltpu.CompilerParams(dimension_semantics=sem, vmem_limit_bytes=VMEM_LIMIT)


def _dot(a, b):
    return jnp.dot(a, b, preferred_element_type=F32)


def _dot_nt(a, b):
    return lax.dot_general(a, b, NT_DIMS, preferred_element_type=F32)


def _mod_kernel(c_ref, w_ref, b_ref, o_ref):
    c = c_ref[...]
    c_act = c * jax.nn.sigmoid(c)
    o_ref[...] = _dot(c_act, w_ref[...]) + b_ref[...]


def _modulation(c, w_mod, b_mod):
    depth, d, n = w_mod.shape
    bsz = c.shape[0]
    tn = _tile(n, 1024)
    return pl.pallas_call(
        _mod_kernel,
        grid=(depth, n // tn),
        in_specs=[
            pl.BlockSpec((bsz, d), lambda l, j: (0, 0)),
            pl.BlockSpec((None, d, tn), lambda l, j: (l, 0, j)),
            pl.BlockSpec((None, 1, tn), lambda l, j: (l, 0, j)),
        ],
        out_specs=pl.BlockSpec((None, bsz, tn), lambda l, j: (l, 0, j)),
        out_shape=jax.ShapeDtypeStruct((depth, bsz, n), F32),
        compiler_params=_params("parallel", "parallel"),
        name="adaln_mod",
    )(c, w_mod, b_mod.reshape(depth, 1, n))


def _store_slabs(ref, y):
    rows, ns = y.shape[0], y.shape[1] // LANES
    for s in range(ns):
        ref[pl.ds(s, rows, stride=ns), :] = y[:, s * LANES:(s + 1) * LANES].astype(ref.dtype)


def _load_slabs(ref, rows, dtype):
    ns = ref.shape[0] // rows
    return jnp.concatenate([ref[pl.ds(s, rows, stride=ns), :].astype(dtype) for s in range(ns)], axis=1)


def _norm_kernel(x_ref, g_ref, *rest, modulate, slab):
    x = x_ref[...]
    y = x * lax.rsqrt(jnp.mean(x * x, axis=-1, keepdims=True) + RMS_EPS) * g_ref[...]
    if modulate:
        sh_ref, sc_ref, o_ref = rest
        y = y * (1.0 + sc_ref[...]) + sh_ref[...]
    else:
        (o_ref,) = rest
    if slab:
        _store_slabs(o_ref, y)
    else:
        o_ref[...] = y.astype(o_ref.dtype)


def _norm(x2, g, seq, shift=None, scale=None, out_dtype=BF16, slab=False):
    t, d = x2.shape
    tm = _tile(seq, 512)
    per_b = seq // tm
    in_specs = [pl.BlockSpec((tm, d), lambda i: (i, 0)), pl.BlockSpec((1, d), lambda i: (0, 0))]
    args = [x2, g.reshape(1, d)]
    if shift is not None:
        vec = pl.BlockSpec((None, 1, d), lambda i: (i // per_b, 0, 0))
        in_specs += [vec, vec]
        args += [shift, scale]
    if slab:
        ns = d // LANES
        out_spec = pl.BlockSpec((tm * ns, LANES), lambda i: (i, 0))
        out_shape = jax.ShapeDtypeStruct((t * ns, LANES), out_dtype)
    else:
        out_spec = pl.BlockSpec((tm, d), lambda i: (i, 0))
        out_shape = jax.ShapeDtypeStruct((t, d), out_dtype)
    return pl.pallas_call(
        functools.partial(_norm_kernel, modulate=shift is not None, slab=slab),
        grid=(t // tm,),
        in_specs=in_specs,
        out_specs=out_spec,
        out_shape=out_shape,
        compiler_params=_params("parallel"),
        name="rmsnorm_mod",
    )(*args)


def _rope_cols(x, cos, sin, half):
    outs = []
    for c in range(x.shape[1] // LANES):
        xc = x[:, c * LANES:(c + 1) * LANES]
        if half == LANES // 2:
            partner = pltpu.roll(xc, LANES // 2, 1)
        else:
            lane = lax.broadcasted_iota(jnp.int32, xc.shape, 1)
            low = (lane & (2 * half - 1)) < half
            partner = jnp.where(low, pltpu.roll(xc, LANES - half, 1), pltpu.roll(xc, half, 1))
        outs.append(xc * cos + partner * sin)
    return outs[0] if len(outs) == 1 else jnp.concatenate(outs, axis=1)


def _mm_kernel(h_ref, w_ref, *rest, epi, rope_cols):
    acc = _dot(h_ref[...], w_ref[...])
    if epi in ("rope64", "rope32"):
        cos_ref, sin_ref, o_ref = rest
        half = 64 if epi == "rope64" else 32
        if rope_cols is None:
            acc = _rope_cols(acc, cos_ref[...], sin_ref[...], half)
        else:
            j = pl.program_id(1)
            roped = _rope_cols(acc, cos_ref[...], sin_ref[...], half)
            acc = jnp.where(j < rope_cols, roped, acc)
    elif epi == "sigmoid":
        (o_ref,) = rest
        acc = jax.nn.sigmoid(acc)
    elif epi == "resid":
        x_ref, gt_ref, o_ref = rest
        acc = x_ref[...] + gt_ref[...] * acc
    else:
        (o_ref,) = rest
    o_ref[...] = acc.astype(o_ref.dtype)


def _mm(h, w, *, epi="none", tables=None, resid=None, seq=None, out_dtype=BF16, rope_cols=None,
        tm_pref=1024, tn_pref=512):
    t, k = h.shape
    n = w.shape[1]
    tm = _tile(t if seq is None else seq, tm_pref)
    tn = _tile(n, tn_pref)
    in_specs = [pl.BlockSpec((tm, k), lambda i, j: (i, 0)), pl.BlockSpec((k, tn), lambda i, j: (0, j))]
    args = [h, w]
    if tables is not None:
        spec = pl.BlockSpec((tm, LANES), lambda i, j: (i, 0))
        in_specs += [spec, spec]
        args += list(tables)
    if resid is not None:
        x2, gt = resid
        per_b = seq // tm
        in_specs += [pl.BlockSpec((tm, tn), lambda i, j: (i, j)),
                     pl.BlockSpec((None, 1, tn), lambda i, j: (i // per_b, 0, j))]
        args += [x2, gt]
    return pl.pallas_call(
        functools.partial(_mm_kernel, epi=epi, rope_cols=rope_cols),
        grid=(t // tm, n // tn),
        in_specs=in_specs,
        out_specs=pl.BlockSpec((tm, tn), lambda i, j: (i, j)),
        out_shape=jax.ShapeDtypeStruct((t, n), out_dtype),
        compiler_params=_params("parallel", "parallel"),
        name="proj_" + epi,
    )(*args)


def _kv_kernel(h_ref, wc_ref, g_ref, wk_ref, wv_ref, cos_ref, sin_ref, k_ref, v_ref):
    ckv = _dot(h_ref[...], wc_ref[...])
    y = ckv * lax.rsqrt(jnp.mean(ckv * ckv, axis=-1, keepdims=True) + RMS_EPS) * g_ref[...]
    yb = y.astype(BF16)
    k = _dot(yb, wk_ref[...])
    k_ref[...] = _rope_cols(k, cos_ref[...], sin_ref[...], HEAD_DIM // 2).astype(k_ref.dtype)
    v_ref[...] = _dot(yb, wv_ref[...]).astype(v_ref.dtype)


def _kv_proj(h, w_ckv, g_kv, w_k, w_v, cos, sin):
    t, d = h.shape
    tm = _tile(t, 512)
    full = lambda a: pl.BlockSpec(a.shape, lambda i: (0, 0))
    row = lambda width: pl.BlockSpec((tm, width), lambda i: (i, 0))
    g2 = g_kv.reshape(1, KV_RANK)
    return pl.pallas_call(
        _kv_kernel,
        grid=(t // tm,),
        in_specs=[row(d), full(w_ckv), full(g2), full(w_k), full(w_v), row(LANES), row(LANES)],
        out_specs=[row(A_Q_WIDTH), row(A_Q_WIDTH)],
        out_shape=[jax.ShapeDtypeStruct((t, A_Q_WIDTH), BF16)] * 2,
        compiler_params=_params("parallel"),
        name="kv_latent",
    )(h, w_ckv, g2, w_k, w_v, cos, sin)


def _idx_kernel(qi_ref, ki_ref, w_ref, o_ref, key_ref, pos_ref, *, tq, ck, n_sel, nkc, scale):
    qb = pl.program_id(1)
    nchunk = (qb * tq + tq + ck - 1) // ck
    q_idx = qb * tq + lax.broadcasted_iota(jnp.int32, (ck, tq), 1)
    s_loc = lax.broadcasted_iota(jnp.int32, (ck, tq), 0)

    def score_chunk(c, carry):
        off = pl.multiple_of(c * ck, ck)
        k = ki_ref[pl.ds(off, ck), :]
        acc = jnp.zeros((ck, tq), F32)
        for h in range(IDX_HEADS):
            d = _dot_nt(k, qi_ref[:, h * IDX_DIM:(h + 1) * IDX_DIM])
            acc = acc + jnp.maximum(d, 0.0) * w_ref[h:h + 1, :]
        bits = pltpu.bitcast(acc * scale, jnp.int32)
        key = jnp.where(bits < 0, bits ^ jnp.int32(0x7FFFFFFF), bits)
        key_ref[pl.ds(off, ck), :] = jnp.where(off + s_loc <= q_idx, key, jnp.int32(INT_MIN))
        return carry

    lax.fori_loop(0, nchunk, score_chunk, 0)

    def bit_step(i, prefix):
        cand_u = prefix | lax.shift_left(jnp.int32(1), 31 - i)
        cand = cand_u ^ jnp.int32(INT_MIN)

        def count_chunk(c, cnt):
            off = pl.multiple_of(c * ck, ck)
            ge = key_ref[pl.ds(off, ck), :] >= cand
            return cnt + jnp.sum(ge.astype(jnp.int32), axis=0, keepdims=True)

        cnt = lax.fori_loop(0, nchunk, count_chunk, jnp.zeros((1, tq), jnp.int32))
        return jnp.where(cnt >= n_sel, cand_u, prefix)

    prefix = lax.fori_loop(0, 32, bit_step, jnp.zeros((1, tq), jnp.int32))
    thr = jnp.maximum(prefix ^ jnp.int32(INT_MIN), jnp.int32(INT_MIN + 1))

    def count_keys(pred):
        def chunk(c, cnt):
            off = pl.multiple_of(c * ck, ck)
            hit = pred(key_ref[pl.ds(off, ck), :], off + s_loc)
            return cnt + jnp.sum(hit.astype(jnp.int32), axis=0, keepdims=True)

        return lax.fori_loop(0, nchunk, chunk, jnp.zeros((1, tq), jnp.int32))

    pos_ref[...] = jnp.full((1, tq), nkc * ck, jnp.int32)
    tied = count_keys(lambda key, s_idx: key >= thr) > n_sel

    @pl.when(jnp.max(tied.astype(jnp.int32)) > 0)
    def _():
        need = n_sel - count_keys(lambda key, s_idx: key > thr)
        n_bits = max((nkc * ck - 1).bit_length(), 1)

        def pos_step(i, lo):
            cand = lo + lax.shift_left(jnp.int32(1), n_bits - 1 - i)
            below = count_keys(lambda key, s_idx: (key == thr) & (s_idx < cand))
            return jnp.where(below < need, cand, lo)

        lo = lax.fori_loop(0, n_bits, pos_step, jnp.zeros((1, tq), jnp.int32))
        pos_ref[...] = jnp.where(tied, lo, pos_ref[...])

    pos_max = pos_ref[...]

    def write_chunk(c, carry):
        off = pl.multiple_of(c * ck, ck)
        key = key_ref[pl.ds(off, ck), :]
        sel = (key > thr) | ((key == thr) & (off + s_loc <= pos_max))
        o_ref[c] = jnp.where(sel, 0.0, NEG).astype(F32).T.astype(o_ref.dtype)
        return carry

    lax.fori_loop(0, nchunk, write_chunk, 0)

    def fill_chunk(c, carry):
        o_ref[c] = jnp.full((tq, ck), NEG, o_ref.dtype)
        return carry

    lax.fori_loop(nchunk, nkc, fill_chunk, 0)


def _index_select(qi, ki, w_t, n_sel, ck):
    bsz, seq, _ = qi.shape
    tq = _tile(seq, LANES)
    nkc = seq // ck
    scale = (IDX_DIM ** -0.5) * (IDX_HEADS ** -0.5)
    return pl.pallas_call(
        functools.partial(_idx_kernel, tq=tq, ck=ck, n_sel=n_sel, nkc=nkc, scale=scale),
        grid=(bsz, seq // tq),
        in_specs=[
            pl.BlockSpec((None, tq, IDX_Q_WIDTH), lambda b, q: (b, q, 0)),
            pl.BlockSpec((None, seq, IDX_DIM), lambda b, q: (b, 0, 0)),
            pl.BlockSpec((None, IDX_HEADS, tq), lambda b, q: (b, 0, q)),
        ],
        out_specs=pl.BlockSpec((None, nkc, tq, ck), lambda b, q: (b, 0, q, 0)),
        out_shape=jax.ShapeDtypeStruct((bsz, nkc, seq, ck), BF16),
        scratch_shapes=[pltpu.VMEM((seq, tq), jnp.int32), pltpu.VMEM((1, tq), jnp.int32)],
        compiler_params=_params("parallel", "parallel"),
        name="indexer_select",
    )(qi, ki, w_t)


def _dsa_kernel(q_ref, k_ref, v_ref, b_ref, o_ref, m_ref, l_ref, acc_ref, *, tq, tk, scale):
    qb = pl.program_id(1)
    kb = pl.program_id(2)
    last = ((qb + 1) * tq - 1) // tk

    @pl.when(kb == 0)
    def _():
        m_ref[...] = jnp.full(m_ref.shape, NEG, F32)
        l_ref[...] = jnp.zeros(l_ref.shape, F32)
        acc_ref[...] = jnp.zeros(acc_ref.shape, F32)

    @pl.when(kb <= last)
    def _():
        bias = b_ref[...].astype(F32)
        for h in range(A_HEADS):
            sl = slice(h * HEAD_DIM, (h + 1) * HEAD_DIM)
            s = _dot_nt(q_ref[:, sl], k_ref[:, sl]) * scale + bias
            m_prev = m_ref[h]
            m_new = jnp.maximum(m_prev, jnp.max(s, axis=1, keepdims=True))
            alpha = jnp.exp(m_prev - m_new)
            p = jnp.exp(s - m_new[:, :1])
            l_ref[h] = alpha * l_ref[h] + jnp.sum(p, axis=1, keepdims=True)
            m_ref[h] = m_new
            acc_ref[:, sl] = acc_ref[:, sl] * alpha + _dot(p.astype(BF16), v_ref[:, sl])

    @pl.when(kb == last)
    def _():
        for h in range(A_HEADS):
            sl = slice(h * HEAD_DIM, (h + 1) * HEAD_DIM)
            o_ref[:, sl] = (acc_ref[:, sl] / l_ref[h]).astype(o_ref.dtype)


def _dsa_attention(q, k, v, bias, tk):
    bsz, seq, width = q.shape
    tq = _tile(seq, 256)
    last = lambda qb: ((qb + 1) * tq - 1) // tk
    kv_spec = pl.BlockSpec((None, tk, width), lambda b, qb, kb: (b, jnp.minimum(kb, last(qb)), 0))
    return pl.pallas_call(
        functools.partial(_dsa_kernel, tq=tq, tk=tk, scale=HEAD_DIM ** -0.5),
        grid=(bsz, seq // tq, seq // tk),
        in_specs=[
            pl.BlockSpec((None, tq, width), lambda b, qb, kb: (b, qb, 0)),
            kv_spec,
            kv_spec,
            pl.BlockSpec((None, None, tq, tk), lambda b, qb, kb: (b, jnp.minimum(kb, last(qb)), qb, 0)),
        ],
        out_specs=pl.BlockSpec((None, tq, width), lambda b, qb, kb: (b, qb, 0)),
        out_shape=jax.ShapeDtypeStruct((bsz, seq, width), BF16),
        scratch_shapes=[
            pltpu.VMEM((A_HEADS, tq, LANES), F32),
            pltpu.VMEM((A_HEADS, tq, LANES), F32),
            pltpu.VMEM((tq, width), F32),
        ],
        compiler_params=_params("parallel", "parallel", "arbitrary"),
        name="dsa_flash",
    )(q, k, v, bias)


def _dil_kernel(q_ref, kc_ref, kp_ref, vc_ref, vp_ref, o_ref, lse_ref, *, w, scale):
    i = pl.program_id(2)
    row = lax.broadcasted_iota(jnp.int32, (w, w), 0)
    col = lax.broadcasted_iota(jnp.int32, (w, w), 1)
    prev_ok = (col >= row) & (i > 0)
    cur_ok = col <= row
    for hh in range(B_HEADS_PER_GROUP):
        sl = slice(hh * HEAD_DIM, (hh + 1) * HEAD_DIM)
        q = q_ref[:, sl]
        sp = jnp.where(prev_ok, _dot_nt(q, kp_ref[:, sl]) * scale, NEG)
        sc = jnp.where(cur_ok, _dot_nt(q, kc_ref[:, sl]) * scale, NEG)
        m = jnp.maximum(jnp.max(sp, axis=1, keepdims=True), jnp.max(sc, axis=1, keepdims=True))
        pp = jnp.exp(sp - m)
        pc = jnp.exp(sc - m)
        l = jnp.sum(pp, axis=1, keepdims=True) + jnp.sum(pc, axis=1, keepdims=True)
        o = _dot(pp.astype(BF16), vp_ref[:, sl]) + _dot(pc.astype(BF16), vc_ref[:, sl])
        o_ref[:, sl] = o / l
        lse_ref[:, sl] = jnp.broadcast_to(m + jnp.log(l), (w, HEAD_DIM))


def _dilated_group(qkv, window, dil):
    bsz, seq, width = qkv.shape
    w = window // dil
    sub = seq // dil
    assert sub % w == 0
    nblk = width // B_OUT_WIDTH
    view = qkv.reshape(bsz, sub, dil * width)
    col = lambda which: (lambda b, r, i: (b, i, r * nblk + which))
    colp = lambda which: (lambda b, r, i: (b, jnp.maximum(i - 1, 0), r * nblk + which))
    blk = lambda f: pl.BlockSpec((None, w, B_OUT_WIDTH), f)
    out_spec = pl.BlockSpec((None, w, B_OUT_WIDTH), lambda b, r, i: (b, i, r))
    o, lse = pl.pallas_call(
        functools.partial(_dil_kernel, w=w, scale=HEAD_DIM ** -0.5),
        grid=(bsz, dil, sub // w),
        in_specs=[blk(col(0)), blk(col(1)), blk(colp(1)), blk(col(2)), blk(colp(2))],
        out_specs=[out_spec, out_spec],
        out_shape=[jax.ShapeDtypeStruct((bsz, sub, dil * B_OUT_WIDTH), F32)] * 2,
        compiler_params=_params("parallel", "parallel", "parallel"),
        name=f"dilated_d{dil}",
    )(view, view, view, view, view)
    return o.reshape(bsz * seq, B_OUT_WIDTH), lse.reshape(bsz * seq, B_OUT_WIDTH)


def _merge_kernel(oa_ref, o0, o1, o2, l0, l1, l2, ga_ref, gb_ref, wa_ref, wb_ref, out_ref):
    m = jnp.maximum(jnp.maximum(l0[...], l1[...]), l2[...])
    e0 = jnp.exp(l0[...] - m)
    e1 = jnp.exp(l1[...] - m)
    e2 = jnp.exp(l2[...] - m)
    ob = (e0 * o0[...] + e1 * o1[...] + e2 * o2[...]) / (e0 + e1 + e2)
    pa = _dot(oa_ref[...], wa_ref[...])
    pb = _dot(ob.astype(BF16), wb_ref[...])
    out_ref[...] = (ga_ref[...].astype(F32) * pa + gb_ref[...].astype(F32) * pb).astype(out_ref.dtype)


def _merge(o_a, o_b, lse_b, gates, w_pa, w_pb):
    t = o_a.shape[0]
    d = w_pa.shape[1]
    tm = _tile(t, 256)
    row = lambda width, cb=0: pl.BlockSpec((tm, width), lambda i: (i, cb))
    full = lambda a: pl.BlockSpec(a.shape, lambda i: (0, 0))
    return pl.pallas_call(
        _merge_kernel,
        grid=(t // tm,),
        in_specs=[row(A_Q_WIDTH)] + [row(B_OUT_WIDTH)] * 6 + [row(d, 0), row(d, 1), full(w_pa), full(w_pb)],
        out_specs=row(d),
        out_shape=jax.ShapeDtypeStruct((t, d), BF16),
        compiler_params=_params("parallel"),
        name="branch_merge",
    )(o_a, *o_b, *lse_b, gates, gates, w_pa, w_pb)


def _router_kernel(h_ref, w_ref, b_ref, idx_ref, wt_ref, *, n_exp):
    lg = _dot(_load_slabs(h_ref, idx_ref.shape[0], BF16), w_ref[...]) + b_ref[...]
    lane = lax.broadcasted_iota(jnp.int32, lg.shape, 1)
    lg = jnp.where(lane < n_exp, lg, -jnp.inf)
    vals, idxs = [], []
    for _ in range(TOP_K):
        m = jnp.max(lg, axis=1, keepdims=True)
        idx = jnp.min(jnp.where(lg == m, lane, LANES), axis=1, keepdims=True)
        vals.append(m)
        idxs.append(idx)
        lg = jnp.where(lane == idx, -jnp.inf, lg)
    es = [jnp.exp(v - vals[0]) for v in vals]
    tot = es[0] + es[1] + es[2] + es[3]
    idx_out = jnp.zeros(lg.shape, jnp.int32)
    wt_out = jnp.zeros(lg.shape, F32)
    for k in range(TOP_K):
        idx_out = jnp.where(lane == k, idxs[k], idx_out)
        wt_out = jnp.where(lane == k, es[k] / tot, wt_out)
    idx_ref[...] = idx_out
    wt_ref[...] = wt_out


def _router(h, w_r, b_r, n_exp):
    ns = w_r.shape[0] // LANES
    t = h.shape[0] // ns
    tm = _tile(t, 512)
    row = lambda width: pl.BlockSpec((tm, width), lambda i: (i, 0))
    full = lambda a: pl.BlockSpec(a.shape, lambda i: (0, 0))
    return pl.pallas_call(
        functools.partial(_router_kernel, n_exp=n_exp),
        grid=(t // tm,),
        in_specs=[pl.BlockSpec((tm * ns, LANES), lambda i: (i, 0)), full(w_r), full(b_r)],
        out_specs=[row(LANES), row(LANES)],
        out_shape=[jax.ShapeDtypeStruct((t, LANES), jnp.int32), jax.ShapeDtypeStruct((t, LANES), F32)],
        compiler_params=_params("parallel"),
        name="router_top4",
    )(h, w_r, b_r)


def _row_copy(src_hbm, dst_ref, sem, src_row, dst_row):
    ns = src_hbm.shape[1]
    dst = dst_ref.at[pl.ds(pl.multiple_of(dst_row * ns, ns), ns)]
    return pltpu.make_async_copy(src_hbm.at[src_row], dst, sem)


def _gather_start(idx_ref, src_hbm, dst_ref, sem, n, priority):
    def body(r, carry):
        _row_copy(src_hbm, dst_ref, sem, idx_ref[0, r], r).start(priority=priority)
        return carry

    lax.fori_loop(0, n, body, 0, unroll=GATHER_UNROLL)


def _gather_wait(src_hbm, dst_ref, sem, n):
    def body(r, carry):
        _row_copy(src_hbm, dst_ref, sem, 0, r).wait()
        return carry

    lax.fori_loop(0, n, body, 0, unroll=GATHER_UNROLL)


def _gather_pipelined(i, n_steps, idx_ref, idx_next_ref, src_hbm, buf_ref, sem, n, priority=0):
    slot = i % 2

    @pl.when(i == 0)
    def _():
        _gather_start(idx_ref, src_hbm, buf_ref.at[0], sem.at[0], n, priority)

    @pl.when(i + 1 < n_steps)
    def _():
        _gather_start(idx_next_ref, src_hbm, buf_ref.at[1 - slot], sem.at[1 - slot], n, priority)

    _gather_wait(src_hbm, buf_ref.at[slot], sem.at[slot], n)
    return slot


def _deint_kernel(w_ref, p_ref, o_ref):
    p = p_ref[...]
    for g in range(w_ref.shape[1] // (2 * LANES)):
        sl = slice(g * 2 * LANES, (g + 1) * 2 * LANES)
        o_ref[:, sl] = _dot(w_ref[:, sl].astype(BF16), p).astype(o_ref.dtype)


def _deinterleave(w, layer):
    _, n_exp, d, n = w.shape
    tn = _tile(n, EXPERT_CHUNK_COLS)
    k = lax.broadcasted_iota(jnp.int32, (2 * LANES, 2 * LANES), 0)
    col = lax.broadcasted_iota(jnp.int32, (2 * LANES, 2 * LANES), 1)
    perm = (k == 2 * (col % LANES) + col // LANES).astype(BF16)
    return pl.pallas_call(
        _deint_kernel,
        grid=(n_exp, n // tn),
        in_specs=[pl.BlockSpec((None, None, d, tn), lambda e, j: (layer, e, 0, j)),
                  pl.BlockSpec(perm.shape, lambda e, j: (0, 0))],
        out_specs=pl.BlockSpec((None, None, d, tn), lambda e, j: (e, j, 0, 0)),
        out_shape=jax.ShapeDtypeStruct((n_exp, n // tn, d, tn), BF16),
        compiler_params=_params("parallel", "parallel"),
        name="moe_w1_layout",
    )(w, perm)


def _cast_kernel(w_ref, o_ref):
    o_ref[...] = w_ref[...].astype(o_ref.dtype)


def _cast_layer_bf16(w, layer):
    _, n_exp, f, d = w.shape
    tf = _tile(f, 512)
    return pl.pallas_call(
        _cast_kernel,
        grid=(n_exp, f // tf),
        in_specs=[pl.BlockSpec((None, None, tf, d), lambda e, j: (layer, e, j, 0))],
        out_specs=pl.BlockSpec((None, tf, d), lambda e, j: (e, j, 0)),
        out_shape=jax.ShapeDtypeStruct((n_exp, f, d), BF16),
        compiler_params=_params("parallel", "parallel"),
        name="moe_w2_cast",
    )(w)


def _expert_kernel(te_ref, tv_ref, tok_ref, tok_next_ref, h_hbm, w1_ref, b1_ref, w2_ref, b2_ref, y_ref,
                   xg_ref, xb_ref, acc_ref, sem, *, tm):
    i = pl.program_id(0)
    j = pl.program_id(1)
    nj = pl.num_programs(1)
    valid = tv_ref[i] > 0

    @pl.when(j == 0)
    def _():
        slot = _gather_pipelined(i, pl.num_programs(0), tok_ref, tok_next_ref, h_hbm, xg_ref, sem, tm, priority=1)
        xb_ref[...] = _load_slabs(xg_ref.at[slot], tm, BF16)
        acc_ref[...] = jnp.zeros(acc_ref.shape, F32)

    @pl.when(valid)
    def _():
        a = _dot(xb_ref[...], w1_ref[...]) + b1_ref[...]
        acts = []
        for g in range(a.shape[1] // (2 * LANES)):
            glu = jnp.minimum(a[:, 2 * g * LANES:(2 * g + 1) * LANES], SWIGLU_LIMIT)
            lin = jnp.clip(a[:, (2 * g + 1) * LANES:(2 * g + 2) * LANES], -SWIGLU_LIMIT, SWIGLU_LIMIT)
            acts.append((glu * jax.nn.sigmoid(SWIGLU_ALPHA * glu) * (lin + 1.0)).astype(BF16))
        act = acts[0] if len(acts) == 1 else jnp.concatenate(acts, axis=1)
        acc_ref[...] += _dot(act, w2_ref[...])

    @pl.when(j == nj - 1)
    def _():
        _store_slabs(y_ref, acc_ref[...] + b2_ref[...])


def _expert_ffn(h_slabs, row_token, tile_expert, tile_valid, w1p, b1p, w2, b2, tm):
    n_rows = row_token.shape[0]
    n_tiles = n_rows // tm
    n_exp, nj, d, fc2 = w1p.shape
    fc = fc2 // 2
    ns = d // LANES
    h_slabs = h_slabs.reshape(-1, ns, LANES)
    jj = lambda j, tv, i: jnp.where(tv[i] > 0, j, nj - 1)
    tok = row_token.reshape(n_tiles, 1, tm)
    grid_spec = pltpu.PrefetchScalarGridSpec(
        num_scalar_prefetch=2,
        grid=(n_tiles, nj),
        in_specs=[
            pl.BlockSpec((None, 1, tm), lambda i, j, te, tv: (i, 0, 0), memory_space=pltpu.SMEM),
            pl.BlockSpec((None, 1, tm), lambda i, j, te, tv: (jnp.minimum(i + 1, n_tiles - 1), 0, 0),
                         memory_space=pltpu.SMEM),
            pl.BlockSpec(memory_space=pl.ANY),
            pl.BlockSpec((None, None, d, 2 * fc), lambda i, j, te, tv: (te[i], jj(j, tv, i), 0, 0)),
            pl.BlockSpec((None, 1, 2 * fc), lambda i, j, te, tv: (te[i], 0, jj(j, tv, i))),
            pl.BlockSpec((None, fc, d), lambda i, j, te, tv: (te[i], jj(j, tv, i), 0)),
            pl.BlockSpec((None, 1, d), lambda i, j, te, tv: (te[i], 0, 0)),
        ],
        out_specs=pl.BlockSpec((tm * ns, LANES), lambda i, j, te, tv: (i, 0)),
        scratch_shapes=[pltpu.VMEM((2, tm * ns, LANES), F32), pltpu.VMEM((tm, d), BF16),
                        pltpu.VMEM((tm, d), F32), pltpu.SemaphoreType.DMA((2,))],
    )
    return pl.pallas_call(
        functools.partial(_expert_kernel, tm=tm),
        grid_spec=grid_spec,
        out_shape=jax.ShapeDtypeStruct((n_rows * ns, LANES), F32),
        compiler_params=_params("arbitrary", "arbitrary"),
        name="moe_experts",
    )(tile_expert, tile_valid, tok, tok, h_slabs, w1p, b1p, w2, b2)


def _combine_kernel(pos_ref, pos_next_ref, x_ref, gt_ref, wt_ref, y_hbm, o_ref, buf_ref, sem, *, tc):
    slot = _gather_pipelined(pl.program_id(0), pl.num_programs(0), pos_ref, pos_next_ref, y_hbm, buf_ref, sem,
                             TOP_K * tc)
    ns = x_ref.shape[1] // LANES
    rb = min(tc, COMBINE_ROWS)
    for r0 in range(0, tc, rb):
        wk = [jnp.broadcast_to(wt_ref[r0:r0 + rb, k:k + 1], (rb, LANES)) for k in range(TOP_K)]
        for s in range(ns):
            sl = slice(s * LANES, (s + 1) * LANES)
            out = wk[0] * buf_ref[slot, pl.ds(r0 * ns + s, rb, stride=ns), :]
            for k in range(1, TOP_K):
                out = out + wk[k] * buf_ref[slot, pl.ds((k * tc + r0) * ns + s, rb, stride=ns), :]
            o_ref[r0:r0 + rb, sl] = x_ref[r0:r0 + rb, sl] + gt_ref[:, sl] * out


def _combine(x2, gt, wts, pos, y_sorted, seq):
    t, d = x2.shape
    tc = _tile(seq, 128)
    per_b = seq // tc
    steps = t // tc
    ns = d // LANES
    y_sorted = y_sorted.reshape(-1, ns, LANES)
    pos_tiles = jnp.swapaxes(pos.reshape(steps, tc, TOP_K), 1, 2).reshape(steps, 1, TOP_K * tc)
    idx_spec = lambda f: pl.BlockSpec((None, 1, TOP_K * tc), f, memory_space=pltpu.SMEM)
    return pl.pallas_call(
        functools.partial(_combine_kernel, tc=tc),
        grid=(steps,),
        in_specs=[
            idx_spec(lambda i: (i, 0, 0)),
            idx_spec(lambda i: (jnp.minimum(i + 1, steps - 1), 0, 0)),
            pl.BlockSpec((tc, d), lambda i: (i, 0)),
            pl.BlockSpec((None, 1, d), lambda i: (i // per_b, 0, 0)),
            pl.BlockSpec((tc, LANES), lambda i: (i, 0)),
            pl.BlockSpec(memory_space=pl.ANY),
        ],
        out_specs=pl.BlockSpec((tc, d), lambda i: (i, 0)),
        out_shape=jax.ShapeDtypeStruct((t, d), F32),
        scratch_shapes=[pltpu.VMEM((2, TOP_K * tc * ns, LANES), F32), pltpu.SemaphoreType.DMA((2,))],
        compiler_params=_params("arbitrary"),
        name="moe_combine",
    )(pos_tiles, pos_tiles, x2, gt, wts, y_sorted)


def _routing_plan(top_idx, n_exp, tm):
    t = top_idx.shape[0]
    pairs = t * TOP_K
    e_flat = top_idx.reshape(pairs)
    onehot = (e_flat[:, None] == jnp.arange(n_exp, dtype=jnp.int32)[None, :]).astype(jnp.int32)
    csum = jnp.cumsum(onehot, axis=0)
    rank = jnp.sum(csum * onehot, axis=1) - 1
    counts = csum[-1]
    tiles_e = (counts + tm - 1) // tm
    tile_end = jnp.cumsum(tiles_e)
    tile_start = tile_end - tiles_e
    pos = tile_start[e_flat] * tm + rank
    n_tiles = pairs // tm + n_exp
    row_token = jnp.zeros((n_tiles * tm,), jnp.int32).at[pos].set(jnp.arange(pairs, dtype=jnp.int32) // TOP_K)
    tile_ids = jnp.arange(n_tiles, dtype=jnp.int32)
    owner = jnp.sum((tile_end[None, :] <= tile_ids[:, None]).astype(jnp.int32), axis=1)
    tile_expert = jnp.minimum(owner, n_exp - 1)
    tile_valid = (tile_ids < tile_end[-1]).astype(jnp.int32)
    last_valid = jnp.maximum(tile_end[-1] - 1, 0)
    tile_expert = jnp.where(tile_valid > 0, tile_expert, tile_expert[last_valid])
    return pos.astype(jnp.int32), row_token, tile_expert, tile_valid


def _rope_tables(positions, dim):
    inv_freq = 1.0 / (ROPE_THETA ** (jnp.arange(0, dim, 2, dtype=F32) / dim))
    ang = positions.astype(F32).reshape(-1, 1) * inv_freq
    return jnp.cos(ang), jnp.sin(ang)


def kernel(x, c, positions, w_mod, b_mod, g_norm1, g_norm2, w_in, g_kv, w_kv_up, w_proj_a, w_proj_b, w_out,
           w_router, b_router, w_moe1, b_moe1, w_moe2, b_moe2, g_final):
    bsz, seq, d = x.shape
    depth = w_mod.shape[0]
    n_exp = w_router.shape[-1]
    t = bsz * seq
    n_sel = min(TOPK_MAX, seq // 4)
    ck = _tile(seq, 512)
    tm_moe = 512 if (t * TOP_K) % 512 == 0 else 64

    cos, sin = _rope_tables(positions, HEAD_DIM)
    cos_a = jnp.concatenate([cos, cos], axis=1)
    sin_a = jnp.concatenate([-sin, sin], axis=1)
    cos_i, sin_i = _rope_tables(positions, IDX_DIM)
    cos_q = jnp.tile(cos_i, (1, LANES // (IDX_DIM // 2)))
    sin_q = jnp.tile(jnp.concatenate([-sin_i, sin_i], axis=1), (1, LANES // IDX_DIM))
    lane_is_k = (jnp.arange(LANES) < IDX_DIM)[None, :]
    cos_k = jnp.where(lane_is_k, cos_q, 1.0)
    sin_k = jnp.where(lane_is_k, sin_q, 0.0)

    mod = _modulation(c, w_mod, b_mod)
    x2 = x.reshape(t, d)

    o1, o2, o3, o4, o5, o6 = (A_Q_WIDTH, A_Q_WIDTH + KV_RANK, A_Q_WIDTH + KV_RANK + IDX_Q_WIDTH,
                              A_Q_WIDTH + KV_RANK + IDX_Q_WIDTH + IDX_DIM,
                              A_Q_WIDTH + KV_RANK + IDX_Q_WIDTH + IDX_DIM + IDX_HEADS,
                              A_Q_WIDTH + KV_RANK + IDX_Q_WIDTH + IDX_DIM + IDX_HEADS + B_QKV_WIDTH)

    for l in range(depth):
        mvec = lambda n: mod[l, :, n * d:(n + 1) * d].reshape(bsz, 1, d)
        sh1, sc1, gt1, sh2, sc2, gt2 = (mvec(n) for n in range(N_MOD))

        wl = w_in[l]
        w_qa = wl[:, :o1].astype(BF16)
        w_ckv = wl[:, o1:o2].astype(BF16)
        w_qi = wl[:, o2:o3].astype(BF16)
        w_kw = jnp.pad(wl[:, o3:o5], ((0, 0), (0, LANES - (o5 - o3)))).astype(BF16)
        w_qkvb = wl[:, o5:o6].reshape(d, 3, B_GROUPS, B_OUT_WIDTH)
        w_b = [w_qkvb[:, :, g].reshape(d, 3 * B_OUT_WIDTH).astype(BF16) for g in range(B_GROUPS)]
        w_gate = wl[:, o6:].astype(BF16)
        wkv = w_kv_up[l].reshape(KV_RANK, A_HEADS, 2, HEAD_DIM)
        w_k = wkv[:, :, 0].reshape(KV_RANK, A_Q_WIDTH).astype(BF16)
        w_v = wkv[:, :, 1].reshape(KV_RANK, A_Q_WIDTH).astype(BF16)

        h = _norm(x2, g_norm1[l], seq, sh1, sc1)
        qa = _mm(h, w_qa, epi="rope64", tables=(cos_a, sin_a))
        ka, va = _kv_proj(h, w_ckv, g_kv[l], w_k, w_v, cos_a, sin_a)
        qi = _mm(h, w_qi, epi="rope32", tables=(cos_q, sin_q))
        kw = _mm(h, w_kw, epi="rope32", tables=(cos_k, sin_k), out_dtype=F32)
        ki = kw[:, :IDX_DIM].astype(BF16).reshape(bsz, seq, IDX_DIM)
        w_t = jnp.swapaxes(kw[:, IDX_DIM:IDX_DIM + IDX_HEADS].reshape(bsz, seq, IDX_HEADS), 1, 2)
        qkv_b = [_mm(h, w_b[g], epi="rope64", tables=(cos_a, sin_a), rope_cols=2, tn_pref=B_OUT_WIDTH)
                 for g in range(B_GROUPS)]
        gates = _mm(h, w_gate, epi="sigmoid")

        bias = _index_select(qi.reshape(bsz, seq, IDX_Q_WIDTH), ki, w_t, n_sel, ck)
        o_a = _dsa_attention(qa.reshape(bsz, seq, A_Q_WIDTH), ka.reshape(bsz, seq, A_Q_WIDTH),
                             va.reshape(bsz, seq, A_Q_WIDTH), bias, ck)
        o_b, lse_b = zip(*[_dilated_group(qkv_b[g].reshape(bsz, seq, 3 * B_OUT_WIDTH), win, dil)
                           for g, (win, dil) in enumerate(B_PATTERNS)])
        merged = _merge(o_a.reshape(t, A_Q_WIDTH), o_b, lse_b, gates,
                        w_proj_a[l].astype(BF16), w_proj_b[l].astype(BF16))
        x2 = _mm(merged, w_out[l].astype(BF16), epi="resid", resid=(x2, gt1), seq=seq, out_dtype=F32,
                 tm_pref=512)

        h2 = _norm(x2, g_norm2[l], seq, sh2, sc2, out_dtype=F32, slab=True)
        w_r = jnp.pad(w_router[l], ((0, 0), (0, LANES - n_exp))).astype(BF16)
        b_r = jnp.pad(b_router[l], (0, LANES - n_exp)).reshape(1, LANES)
        top_idx, top_w = _router(h2, w_r, b_r, n_exp)
        pos, row_token, tile_expert, tile_valid = _routing_plan(top_idx[:, :TOP_K], n_exp, tm_moe)
        f2 = w_moe1.shape[-1]
        b1p = jnp.swapaxes(b_moe1[l].reshape(n_exp, f2 // (2 * LANES), LANES, 2), 2, 3).reshape(n_exp, 1, f2)
        y_sorted = _expert_ffn(h2, row_token, tile_expert, tile_valid, _deinterleave(w_moe1, l), b1p,
                               _cast_layer_bf16(w_moe2, l), b_moe2[l].reshape(n_exp, 1, d), tm_moe)
        x2 = _combine(x2, gt2, top_w, pos, y_sorted, seq)

    return _norm(x2, g_final, seq, out_dtype=x.dtype).reshape(bsz, seq, d)
```

```python
import functools

import jax
import jax.numpy as jnp
from jax import lax
from jax.experimental import pallas as pl
from jax.experimental.pallas import tpu as pltpu

HEAD_DIM = 128
ROPE_THETA = 10000.0
RMS_EPS = 1e-5
A_HEADS = 8
KV_RANK = 512
IDX_HEADS = 16
IDX_DIM = 64
TOPK_MAX = 256
B_PATTERNS = ((128, 1), (512, 4), (2048, 16))
B_GROUPS = 3
B_HEADS_PER_GROUP = 4
N_MOD = 6
TOP_K = 4
SWIGLU_LIMIT = 7.0
SWIGLU_ALPHA = 1.702

LANES = 128
NEG = -1e30
INT_MIN = -(2 ** 31)
VMEM_LIMIT = 48 * 1024 * 1024
GATHER_UNROLL = 8
COMBINE_ROWS = 32
EXPERT_CHUNK_COLS = 1024

A_Q_WIDTH = A_HEADS * HEAD_DIM
IDX_Q_WIDTH = IDX_HEADS * IDX_DIM
B_OUT_WIDTH = B_HEADS_PER_GROUP * HEAD_DIM
B_QKV_WIDTH = 3 * B_GROUPS * B_OUT_WIDTH

F32 = jnp.float32
BF16 = jnp.bfloat16
NT_DIMS = (((1,), (1,)), ((), ()))


def _tile(n, pref):
    return pref if n % pref == 0 else n


def _params(*sem):
    return pltpu.CompilerParams(dimension_semantics=sem, vmem_limit_bytes=VMEM_LIMIT)


def _dot(a, b):
    return jnp.dot(a, b, preferred_element_type=F32)


def _dot_nt(a, b):
    return lax.dot_general(a, b, NT_DIMS, preferred_element_type=F32)


def _mod_kernel(c_ref, w_ref, b_ref, o_ref):
    c = c_ref[...]
    c_act = c * jax.nn.sigmoid(c)
    o_ref[...] = _dot(c_act, w_ref[...]) + b_ref[...]


def _modulation(c, w_mod, b_mod):
    depth, d, n = w_mod.shape
    bsz = c.shape[0]
    tn = _tile(n, 1024)
    return pl.pallas_call(
        _mod_kernel,
        grid=(depth, n // tn),
        in_specs=[
            pl.BlockSpec((bsz, d), lambda l, j: (0, 0)),
            pl.BlockSpec((None, d, tn), lambda l, j: (l, 0, j)),
            pl.BlockSpec((None, 1, tn), lambda l, j: (l, 0, j)),
        ],
        out_specs=pl.BlockSpec((None, bsz, tn), lambda l, j: (l, 0, j)),
        out_shape=jax.ShapeDtypeStruct((depth, bsz, n), F32),
        compiler_params=_params("parallel", "parallel"),
        name="adaln_mod",
    )(c, w_mod, b_mod.reshape(depth, 1, n))


def _store_slabs(ref, y):
    rows, ns = y.shape[0], y.shape[1] // LANES
    for s in range(ns):
        ref[pl.ds(s, rows, stride=ns), :] = y[:, s * LANES:(s + 1) * LANES].astype(ref.dtype)


def _load_slabs(ref, rows, dtype):
    ns = ref.shape[0] // rows
    return jnp.concatenate([ref[pl.ds(s, rows, stride=ns), :].astype(dtype) for s in range(ns)], axis=1)


def _norm_kernel(x_ref, g_ref, *rest, modulate, slab):
    x = x_ref[...]
    y = x * lax.rsqrt(jnp.mean(x * x, axis=-1, keepdims=True) + RMS_EPS) * g_ref[...]
    if modulate:
        sh_ref, sc_ref, o_ref = rest
        y = y * (1.0 + sc_ref[...]) + sh_ref[...]
    else:
        (o_ref,) = rest
    if slab:
        _store_slabs(o_ref, y)
    else:
        o_ref[...] = y.astype(o_ref.dtype)


def _norm(x2, g, seq, shift=None, scale=None, out_dtype=BF16, slab=False):
    t, d = x2.shape
    tm = _tile(seq, 512)
    per_b = seq // tm
    in_specs = [pl.BlockSpec((tm, d), lambda i: (i, 0)), pl.BlockSpec((1, d), lambda i: (0, 0))]
    args = [x2, g.reshape(1, d)]
    if shift is not None:
        vec = pl.BlockSpec((None, 1, d), lambda i: (i // per_b, 0, 0))
        in_specs += [vec, vec]
        args += [shift, scale]
    if slab:
        ns = d // LANES
        out_spec = pl.BlockSpec((tm * ns, LANES), lambda i: (i, 0))
        out_shape = jax.ShapeDtypeStruct((t * ns, LANES), out_dtype)
    else:
        out_spec = pl.BlockSpec((tm, d), lambda i: (i, 0))
        out_shape = jax.ShapeDtypeStruct((t, d), out_dtype)
    return pl.pallas_call(
        functools.partial(_norm_kernel, modulate=shift is not None, slab=slab),
        grid=(t // tm,),
        in_specs=in_specs,
        out_specs=out_spec,
        out_shape=out_shape,
        compiler_params=_params("parallel"),
        name="rmsnorm_mod",
    )(*args)


def _rope_cols(x, cos, sin, half):
    outs = []
    for c in range(x.shape[1] // LANES):
        xc = x[:, c * LANES:(c + 1) * LANES]
        if half == LANES // 2:
            partner = pltpu.roll(xc, LANES // 2, 1)
        else:
            lane = lax.broadcasted_iota(jnp.int32, xc.shape, 1)
            low = (lane & (2 * half - 1)) < half
            partner = jnp.where(low, pltpu.roll(xc, LANES - half, 1), pltpu.roll(xc, half, 1))
        outs.append(xc * cos + partner * sin)
    return outs[0] if len(outs) == 1 else jnp.concatenate(outs, axis=1)


def _mm_kernel(h_ref, w_ref, *rest, epi, rope_cols):
    acc = _dot(h_ref[...], w_ref[...])
    if epi in ("rope64", "rope32"):
        cos_ref, sin_ref, o_ref = rest
        half = 64 if epi == "rope64" else 32
        if rope_cols is None:
            acc = _rope_cols(acc, cos_ref[...], sin_ref[...], half)
        else:
            j = pl.program_id(1)
            roped = _rope_cols(acc, cos_ref[...], sin_ref[...], half)
            acc = jnp.where(j < rope_cols, roped, acc)
    elif epi == "sigmoid":
        (o_ref,) = rest
        acc = jax.nn.sigmoid(acc)
    elif epi == "resid":
        x_ref, gt_ref, o_ref = rest
        acc = x_ref[...] + gt_ref[...] * acc
    else:
        (o_ref,) = rest
    o_ref[...] = acc.astype(o_ref.dtype)


def _mm(h, w, *, epi="none", tables=None, resid=None, seq=None, out_dtype=BF16, rope_cols=None,
        tm_pref=1024, tn_pref=512):
    t, k = h.shape
    n = w.shape[1]
    tm = _tile(t if seq is None else seq, tm_pref)
    tn = _tile(n, tn_pref)
    in_specs = [pl.BlockSpec((tm, k), lambda i, j: (i, 0)), pl.BlockSpec((k, tn), lambda i, j: (0, j))]
    args = [h, w]
    if tables is not None:
        spec = pl.BlockSpec((tm, LANES), lambda i, j: (i, 0))
        in_specs += [spec, spec]
        args += list(tables)
    if resid is not None:
        x2, gt = resid
        per_b = seq // tm
        in_specs += [pl.BlockSpec((tm, tn), lambda i, j: (i, j)),
                     pl.BlockSpec((None, 1, tn), lambda i, j: (i // per_b, 0, j))]
        args += [x2, gt]
    return pl.pallas_call(
        functools.partial(_mm_kernel, epi=epi, rope_cols=rope_cols),
        grid=(t // tm, n // tn),
        in_specs=in_specs,
        out_specs=pl.BlockSpec((tm, tn), lambda i, j: (i, j)),
        out_shape=jax.ShapeDtypeStruct((t, n), out_dtype),
        compiler_params=_params("parallel", "parallel"),
        name="proj_" + epi,
    )(*args)


def _kv_kernel(h_ref, wc_ref, g_ref, wk_ref, wv_ref, cos_ref, sin_ref, k_ref, v_ref):
    ckv = _dot(h_ref[...], wc_ref[...])
    y = ckv * lax.rsqrt(jnp.mean(ckv * ckv, axis=-1, keepdims=True) + RMS_EPS) * g_ref[...]
    yb = y.astype(BF16)
    k = _dot(yb, wk_ref[...])
    k_ref[...] = _rope_cols(k, cos_ref[...], sin_ref[...], HEAD_DIM // 2).astype(k_ref.dtype)
    v_ref[...] = _dot(yb, wv_ref[...]).astype(v_ref.dtype)


def _kv_proj(h, w_ckv, g_kv, w_k, w_v, cos, sin):
    t, d = h.shape
    tm = _tile(t, 512)
    full = lambda a: pl.BlockSpec(a.shape, lambda i: (0, 0))
    row = lambda width: pl.BlockSpec((tm, width), lambda i: (i, 0))
    g2 = g_kv.reshape(1, KV_RANK)
    return pl.pallas_call(
        _kv_kernel,
        grid=(t // tm,),
        in_specs=[row(d), full(w_ckv), full(g2), full(w_k), full(w_v), row(LANES), row(LANES)],
        out_specs=[row(A_Q_WIDTH), row(A_Q_WIDTH)],
        out_shape=[jax.ShapeDtypeStruct((t, A_Q_WIDTH), BF16)] * 2,
        compiler_params=_params("parallel"),
        name="kv_latent",
    )(h, w_ckv, g2, w_k, w_v, cos, sin)


def _idx_kernel(qi_ref, ki_ref, w_ref, o_ref, key_ref, pos_ref, *, tq, ck, n_sel, nkc, scale):
    qb = pl.program_id(1)
    nchunk = (qb * tq + tq + ck - 1) // ck
    q_idx = qb * tq + lax.broadcasted_iota(jnp.int32, (ck, tq), 1)
    s_loc = lax.broadcasted_iota(jnp.int32, (ck, tq), 0)

    def score_chunk(c, carry):
        off = pl.multiple_of(c * ck, ck)
        k = ki_ref[pl.ds(off, ck), :]
        acc = jnp.zeros((ck, tq), F32)
        for h in range(IDX_HEADS):
            d = _dot_nt(k, qi_ref[:, h * IDX_DIM:(h + 1) * IDX_DIM])
            acc = acc + jnp.maximum(d, 0.0) * w_ref[h:h + 1, :]
        bits = pltpu.bitcast(acc * scale, jnp.int32)
        key = jnp.where(bits < 0, bits ^ jnp.int32(0x7FFFFFFF), bits)
        key_ref[pl.ds(off, ck), :] = jnp.where(off + s_loc <= q_idx, key, jnp.int32(INT_MIN))
        return carry

    lax.fori_loop(0, nchunk, score_chunk, 0)

    def bit_step(i, prefix):
        cand_u = prefix | lax.shift_left(jnp.int32(1), 31 - i)
        cand = cand_u ^ jnp.int32(INT_MIN)

        def count_chunk(c, cnt):
            off = pl.multiple_of(c * ck, ck)
            ge = key_ref[pl.ds(off, ck), :] >= cand
            return cnt + jnp.sum(ge.astype(jnp.int32), axis=0, keepdims=True)

        cnt = lax.fori_loop(0, nchunk, count_chunk, jnp.zeros((1, tq), jnp.int32))
        return jnp.where(cnt >= n_sel, cand_u, prefix)

    prefix = lax.fori_loop(0, 32, bit_step, jnp.zeros((1, tq), jnp.int32))
    thr = jnp.maximum(prefix ^ jnp.int32(INT_MIN), jnp.int32(INT_MIN + 1))

    def count_keys(pred):
        def chunk(c, cnt):
            off = pl.multiple_of(c * ck, ck)
            hit = pred(key_ref[pl.ds(off, ck), :], off + s_loc)
            return cnt + jnp.sum(hit.astype(jnp.int32), axis=0, keepdims=True)

        return lax.fori_loop(0, nchunk, chunk, jnp.zeros((1, tq), jnp.int32))

    pos_ref[...] = jnp.full((1, tq), nkc * ck, jnp.int32)
    tied = count_keys(lambda key, s_idx: key >= thr) > n_sel

    @pl.when(jnp.max(tied.astype(jnp.int32)) > 0)
    def _():
        need = n_sel - count_keys(lambda key, s_idx: key > thr)
        n_bits = max((nkc * ck - 1).bit_length(), 1)

        def pos_step(i, lo):
            cand = lo + lax.shift_left(jnp.int32(1), n_bits - 1 - i)
            below = count_keys(lambda key, s_idx: (key == thr) & (s_idx < cand))
            return jnp.where(below < need, cand, lo)

        lo = lax.fori_loop(0, n_bits, pos_step, jnp.zeros((1, tq), jnp.int32))
        pos_ref[...] = jnp.where(tied, lo, pos_ref[...])

    pos_max = pos_ref[...]

    def write_chunk(c, carry):
        off = pl.multiple_of(c * ck, ck)
        key = key_ref[pl.ds(off, ck), :]
        sel = (key > thr) | ((key == thr) & (off + s_loc <= pos_max))
        o_ref[c] = jnp.where(sel, 0.0, NEG).astype(F32).T.astype(o_ref.dtype)
        return carry

    lax.fori_loop(0, nchunk, write_chunk, 0)

    def fill_chunk(c, carry):
        o_ref[c] = jnp.full((tq, ck), NEG, o_ref.dtype)
        return carry

    lax.fori_loop(nchunk, nkc, fill_chunk, 0)


def _index_select(qi, ki, w_t, n_sel, ck):
    bsz, seq, _ = qi.shape
    tq = _tile(seq, LANES)
    nkc = seq // ck
    scale = (IDX_DIM ** -0.5) * (IDX_HEADS ** -0.5)
    return pl.pallas_call(
        functools.partial(_idx_kernel, tq=tq, ck=ck, n_sel=n_sel, nkc=nkc, scale=scale),
        grid=(bsz, seq // tq),
        in_specs=[
            pl.BlockSpec((None, tq, IDX_Q_WIDTH), lambda b, q: (b, q, 0)),
            pl.BlockSpec((None, seq, IDX_DIM), lambda b, q: (b, 0, 0)),
            pl.BlockSpec((None, IDX_HEADS, tq), lambda b, q: (b, 0, q)),
        ],
        out_specs=pl.BlockSpec((None, nkc, tq, ck), lambda b, q: (b, 0, q, 0)),
        out_shape=jax.ShapeDtypeStruct((bsz, nkc, seq, ck), BF16),
        scratch_shapes=[pltpu.VMEM((seq, tq), jnp.int32), pltpu.VMEM((1, tq), jnp.int32)],
        compiler_params=_params("parallel", "parallel"),
        name="indexer_select",
    )(qi, ki, w_t)


def _dsa_kernel(q_ref, k_ref, v_ref, b_ref, o_ref, m_ref, l_ref, acc_ref, *, tq, tk, scale):
    qb = pl.program_id(1)
    kb = pl.program_id(2)
    last = ((qb + 1) * tq - 1) // tk

    @pl.when(kb == 0)
    def _():
        m_ref[...] = jnp.full(m_ref.shape, NEG, F32)
        l_ref[...] = jnp.zeros(l_ref.shape, F32)
        acc_ref[...] = jnp.zeros(acc_ref.shape, F32)

    @pl.when(kb <= last)
    def _():
        bias = b_ref[...].astype(F32)
        for h in range(A_HEADS):
            sl = slice(h * HEAD_DIM, (h + 1) * HEAD_DIM)
            s = _dot_nt(q_ref[:, sl], k_ref[:, sl]) * scale + bias
            m_prev = m_ref[h]
            m_new = jnp.maximum(m_prev, jnp.max(s, axis=1, keepdims=True))
            alpha = jnp.exp(m_prev - m_new)
            p = jnp.exp(s - m_new[:, :1])
            l_ref[h] = alpha * l_ref[h] + jnp.sum(p, axis=1, keepdims=True)
            m_ref[h] = m_new
            acc_ref[:, sl] = acc_ref[:, sl] * alpha + _dot(p.astype(BF16), v_ref[:, sl])

    @pl.when(kb == last)
    def _():
        for h in range(A_HEADS):
            sl = slice(h * HEAD_DIM, (h + 1) * HEAD_DIM)
            o_ref[:, sl] = (acc_ref[:, sl] / l_ref[h]).astype(o_ref.dtype)


def _dsa_attention(q, k, v, bias, tk):
    bsz, seq, width = q.shape
    tq = _tile(seq, 512)
    last = lambda qb: ((qb + 1) * tq - 1) // tk
    kv_spec = pl.BlockSpec((None, tk, width), lambda b, qb, kb: (b, jnp.minimum(kb, last(qb)), 0))
    return pl.pallas_call(
        functools.partial(_dsa_kernel, tq=tq, tk=tk, scale=HEAD_DIM ** -0.5),
        grid=(bsz, seq // tq, seq // tk),
        in_specs=[
            pl.BlockSpec((None, tq, width), lambda b, qb, kb: (b, qb, 0)),
            kv_spec,
            kv_spec,
            pl.BlockSpec((None, None, tq, tk), lambda b, qb, kb: (b, jnp.minimum(kb, last(qb)), qb, 0)),
        ],
        out_specs=pl.BlockSpec((None, tq, width), lambda b, qb, kb: (b, qb, 0)),
        out_shape=jax.ShapeDtypeStruct((bsz, seq, width), BF16),
        scratch_shapes=[
            pltpu.VMEM((A_HEADS, tq, LANES), F32),
            pltpu.VMEM((A_HEADS, tq, LANES), F32),
            pltpu.VMEM((tq, width), F32),
        ],
        compiler_params=_params("parallel", "parallel", "arbitrary"),
        name="dsa_flash",
    )(q, k, v, bias)


def _dil_kernel(q_ref, kc_ref, kp_ref, vc_ref, vp_ref, o_ref, lse_ref, *, w, scale):
    i = pl.program_id(2)
    row = lax.broadcasted_iota(jnp.int32, (w, w), 0)
    col = lax.broadcasted_iota(jnp.int32, (w, w), 1)
    prev_ok = (col >= row) & (i > 0)
    cur_ok = col <= row
    for hh in range(B_HEADS_PER_GROUP):
        sl = slice(hh * HEAD_DIM, (hh + 1) * HEAD_DIM)
        q = q_ref[:, sl]
        sp = jnp.where(prev_ok, _dot_nt(q, kp_ref[:, sl]) * scale, NEG)
        sc = jnp.where(cur_ok, _dot_nt(q, kc_ref[:, sl]) * scale, NEG)
        m = jnp.maximum(jnp.max(sp, axis=1, keepdims=True), jnp.max(sc, axis=1, keepdims=True))
        pp = jnp.exp(sp - m)
        pc = jnp.exp(sc - m)
        l = jnp.sum(pp, axis=1, keepdims=True) + jnp.sum(pc, axis=1, keepdims=True)
        o = _dot(pp.astype(BF16), vp_ref[:, sl]) + _dot(pc.astype(BF16), vc_ref[:, sl])
        o_ref[:, sl] = o / l
        lse_ref[:, sl] = jnp.broadcast_to(m + jnp.log(l), (w, HEAD_DIM))


def _dilated_group(qkv, window, dil):
    bsz, seq, width = qkv.shape
    w = window // dil
    sub = seq // dil
    assert sub % w == 0
    nblk = width // B_OUT_WIDTH
    view = qkv.reshape(bsz, sub, dil * width)
    col = lambda which: (lambda b, r, i: (b, i, r * nblk + which))
    colp = lambda which: (lambda b, r, i: (b, jnp.maximum(i - 1, 0), r * nblk + which))
    blk = lambda f: pl.BlockSpec((None, w, B_OUT_WIDTH), f)
    out_spec = pl.BlockSpec((None, w, B_OUT_WIDTH), lambda b, r, i: (b, i, r))
    o, lse = pl.pallas_call(
        functools.partial(_dil_kernel, w=w, scale=HEAD_DIM ** -0.5),
        grid=(bsz, dil, sub // w),
        in_specs=[blk(col(0)), blk(col(1)), blk(colp(1)), blk(col(2)), blk(colp(2))],
        out_specs=[out_spec, out_spec],
        out_shape=[jax.ShapeDtypeStruct((bsz, sub, dil * B_OUT_WIDTH), F32)] * 2,
        compiler_params=_params("parallel", "parallel", "parallel"),
        name=f"dilated_d{dil}",
    )(view, view, view, view, view)
    return o.reshape(bsz * seq, B_OUT_WIDTH), lse.reshape(bsz * seq, B_OUT_WIDTH)


def _merge_kernel(oa_ref, o0, o1, o2, l0, l1, l2, ga_ref, gb_ref, wa_ref, wb_ref, out_ref):
    m = jnp.maximum(jnp.maximum(l0[...], l1[...]), l2[...])
    e0 = jnp.exp(l0[...] - m)
    e1 = jnp.exp(l1[...] - m)
    e2 = jnp.exp(l2[...] - m)
    ob = (e0 * o0[...] + e1 * o1[...] + e2 * o2[...]) / (e0 + e1 + e2)
    pa = _dot(oa_ref[...], wa_ref[...])
    pb = _dot(ob.astype(BF16), wb_ref[...])
    out_ref[...] = (ga_ref[...].astype(F32) * pa + gb_ref[...].astype(F32) * pb).astype(out_ref.dtype)


def _merge(o_a, o_b, lse_b, gates, w_pa, w_pb):
    t = o_a.shape[0]
    d = w_pa.shape[1]
    tm = _tile(t, 256)
    row = lambda width, cb=0: pl.BlockSpec((tm, width), lambda i: (i, cb))
    full = lambda a: pl.BlockSpec(a.shape, lambda i: (0, 0))
    return pl.pallas_call(
        _merge_kernel,
        grid=(t // tm,),
        in_specs=[row(A_Q_WIDTH)] + [row(B_OUT_WIDTH)] * 6 + [row(d, 0), row(d, 1), full(w_pa), full(w_pb)],
        out_specs=row(d),
        out_shape=jax.ShapeDtypeStruct((t, d), BF16),
        compiler_params=_params("parallel"),
        name="branch_merge",
    )(o_a, *o_b, *lse_b, gates, gates, w_pa, w_pb)


def _router_kernel(h_ref, w_ref, b_ref, idx_ref, wt_ref, *, n_exp):
    lg = _dot(_load_slabs(h_ref, idx_ref.shape[0], BF16), w_ref[...]) + b_ref[...]
    lane = lax.broadcasted_iota(jnp.int32, lg.shape, 1)
    lg = jnp.where(lane < n_exp, lg, -jnp.inf)
    vals, idxs = [], []
    for _ in range(TOP_K):
        m = jnp.max(lg, axis=1, keepdims=True)
        idx = jnp.min(jnp.where(lg == m, lane, LANES), axis=1, keepdims=True)
        vals.append(m)
        idxs.append(idx)
        lg = jnp.where(lane == idx, -jnp.inf, lg)
    es = [jnp.exp(v - vals[0]) for v in vals]
    tot = es[0] + es[1] + es[2] + es[3]
    idx_out = jnp.zeros(lg.shape, jnp.int32)
    wt_out = jnp.zeros(lg.shape, F32)
    for k in range(TOP_K):
        idx_out = jnp.where(lane == k, idxs[k], idx_out)
        wt_out = jnp.where(lane == k, es[k] / tot, wt_out)
    idx_ref[...] = idx_out
    wt_ref[...] = wt_out


def _router(h, w_r, b_r, n_exp):
    ns = w_r.shape[0] // LANES
    t = h.shape[0] // ns
    tm = _tile(t, 512)
    row = lambda width: pl.BlockSpec((tm, width), lambda i: (i, 0))
    full = lambda a: pl.BlockSpec(a.shape, lambda i: (0, 0))
    return pl.pallas_call(
        functools.partial(_router_kernel, n_exp=n_exp),
        grid=(t // tm,),
        in_specs=[pl.BlockSpec((tm * ns, LANES), lambda i: (i, 0)), full(w_r), full(b_r)],
        out_specs=[row(LANES), row(LANES)],
        out_shape=[jax.ShapeDtypeStruct((t, LANES), jnp.int32), jax.ShapeDtypeStruct((t, LANES), F32)],
        compiler_params=_params("parallel"),
        name="router_top4",
    )(h, w_r, b_r)


def _row_copy(src_hbm, dst_ref, sem, src_row, dst_row):
    ns = src_hbm.shape[1]
    dst = dst_ref.at[pl.ds(pl.multiple_of(dst_row * ns, ns), ns)]
    return pltpu.make_async_copy(src_hbm.at[src_row], dst, sem)


def _gather_start(idx_ref, src_hbm, dst_ref, sem, n, priority):
    def body(r, carry):
        _row_copy(src_hbm, dst_ref, sem, idx_ref[0, r], r).start(priority=priority)
        return carry

    lax.fori_loop(0, n, body, 0, unroll=GATHER_UNROLL)


def _gather_wait(src_hbm, dst_ref, sem, n):
    def body(r, carry):
        _row_copy(src_hbm, dst_ref, sem, 0, r).wait()
        return carry

    lax.fori_loop(0, n, body, 0, unroll=GATHER_UNROLL)


def _gather_pipelined(i, n_steps, idx_ref, idx_next_ref, src_hbm, buf_ref, sem, n, priority=0):
    slot = i % 2

    @pl.when(i == 0)
    def _():
        _gather_start(idx_ref, src_hbm, buf_ref.at[0], sem.at[0], n, priority)

    @pl.when(i + 1 < n_steps)
    def _():
        _gather_start(idx_next_ref, src_hbm, buf_ref.at[1 - slot], sem.at[1 - slot], n, priority)

    _gather_wait(src_hbm, buf_ref.at[slot], sem.at[slot], n)
    return slot


def _deint_kernel(w_ref, p_ref, o_ref):
    p = p_ref[...]
    for g in range(w_ref.shape[1] // (2 * LANES)):
        sl = slice(g * 2 * LANES, (g + 1) * 2 * LANES)
        o_ref[:, sl] = _dot(w_ref[:, sl].astype(BF16), p).astype(o_ref.dtype)


def _deinterleave(w, layer):
    _, n_exp, d, n = w.shape
    tn = _tile(n, EXPERT_CHUNK_COLS)
    k = lax.broadcasted_iota(jnp.int32, (2 * LANES, 2 * LANES), 0)
    col = lax.broadcasted_iota(jnp.int32, (2 * LANES, 2 * LANES), 1)
    perm = (k == 2 * (col % LANES) + col // LANES).astype(BF16)
    return pl.pallas_call(
        _deint_kernel,
        grid=(n_exp, n // tn),
        in_specs=[pl.BlockSpec((None, None, d, tn), lambda e, j: (layer, e, 0, j)),
                  pl.BlockSpec(perm.shape, lambda e, j: (0, 0))],
        out_specs=pl.BlockSpec((None, None, d, tn), lambda e, j: (e, j, 0, 0)),
        out_shape=jax.ShapeDtypeStruct((n_exp, n // tn, d, tn), BF16),
        compiler_params=_params("parallel", "parallel"),
        name="moe_w1_layout",
    )(w, perm)


def _cast_kernel(w_ref, o_ref):
    o_ref[...] = w_ref[...].astype(o_ref.dtype)


def _cast_layer_bf16(w, layer):
    _, n_exp, f, d = w.shape
    tf = _tile(f, 512)
    return pl.pallas_call(
        _cast_kernel,
        grid=(n_exp, f // tf),
        in_specs=[pl.BlockSpec((None, None, tf, d), lambda e, j: (layer, e, j, 0))],
        out_specs=pl.BlockSpec((None, tf, d), lambda e, j: (e, j, 0)),
        out_shape=jax.ShapeDtypeStruct((n_exp, f, d), BF16),
        compiler_params=_params("parallel", "parallel"),
        name="moe_w2_cast",
    )(w)


def _expert_kernel(te_ref, tv_ref, tok_ref, tok_next_ref, h_hbm, w1_ref, b1_ref, w2_ref, b2_ref, y_ref,
                   xg_ref, xb_ref, acc_ref, sem, *, tm):
    i = pl.program_id(0)
    j = pl.program_id(1)
    nj = pl.num_programs(1)
    valid = tv_ref[i] > 0

    @pl.when(j == 0)
    def _():
        slot = _gather_pipelined(i, pl.num_programs(0), tok_ref, tok_next_ref, h_hbm, xg_ref, sem, tm, priority=1)
        xb_ref[...] = _load_slabs(xg_ref.at[slot], tm, BF16)
        acc_ref[...] = jnp.zeros(acc_ref.shape, F32)

    @pl.when(valid)
    def _():
        a = _dot(xb_ref[...], w1_ref[...]) + b1_ref[...]
        acts = []
        for g in range(a.shape[1] // (2 * LANES)):
            glu = jnp.minimum(a[:, 2 * g * LANES:(2 * g + 1) * LANES], SWIGLU_LIMIT)
            lin = jnp.clip(a[:, (2 * g + 1) * LANES:(2 * g + 2) * LANES], -SWIGLU_LIMIT, SWIGLU_LIMIT)
            acts.append((glu * jax.nn.sigmoid(SWIGLU_ALPHA * glu) * (lin + 1.0)).astype(BF16))
        act = acts[0] if len(acts) == 1 else jnp.concatenate(acts, axis=1)
        acc_ref[...] += _dot(act, w2_ref[...])

    @pl.when(j == nj - 1)
    def _():
        _store_slabs(y_ref, acc_ref[...] + b2_ref[...])


def _expert_ffn(h_slabs, row_token, tile_expert, tile_valid, w1p, b1p, w2, b2, tm):
    n_rows = row_token.shape[0]
    n_tiles = n_rows // tm
    n_exp, nj, d, fc2 = w1p.shape
    fc = fc2 // 2
    ns = d // LANES
    h_slabs = h_slabs.reshape(-1, ns, LANES)
    jj = lambda j, tv, i: jnp.where(tv[i] > 0, j, nj - 1)
    tok = row_token.reshape(n_tiles, 1, tm)
    grid_spec = pltpu.PrefetchScalarGridSpec(
        num_scalar_prefetch=2,
        grid=(n_tiles, nj),
        in_specs=[
            pl.BlockSpec((None, 1, tm), lambda i, j, te, tv: (i, 0, 0), memory_space=pltpu.SMEM),
            pl.BlockSpec((None, 1, tm), lambda i, j, te, tv: (jnp.minimum(i + 1, n_tiles - 1), 0, 0),
                         memory_space=pltpu.SMEM),
            pl.BlockSpec(memory_space=pl.ANY),
            pl.BlockSpec((None, None, d, 2 * fc), lambda i, j, te, tv: (te[i], jj(j, tv, i), 0, 0)),
            pl.BlockSpec((None, 1, 2 * fc), lambda i, j, te, tv: (te[i], 0, jj(j, tv, i))),
            pl.BlockSpec((None, fc, d), lambda i, j, te, tv: (te[i], jj(j, tv, i), 0)),
            pl.BlockSpec((None, 1, d), lambda i, j, te, tv: (te[i], 0, 0)),
        ],
        out_specs=pl.BlockSpec((tm * ns, LANES), lambda i, j, te, tv: (i, 0)),
        scratch_shapes=[pltpu.VMEM((2, tm * ns, LANES), F32), pltpu.VMEM((tm, d), BF16),
                        pltpu.VMEM((tm, d), F32), pltpu.SemaphoreType.DMA((2,))],
    )
    return pl.pallas_call(
        functools.partial(_expert_kernel, tm=tm),
        grid_spec=grid_spec,
        out_shape=jax.ShapeDtypeStruct((n_rows * ns, LANES), F32),
        compiler_params=_params("arbitrary", "arbitrary"),
        name="moe_experts",
    )(tile_expert, tile_valid, tok, tok, h_slabs, w1p, b1p, w2, b2)


def _combine_kernel(pos_ref, pos_next_ref, x_ref, gt_ref, wt_ref, y_hbm, o_ref, buf_ref, sem, *, tc):
    slot = _gather_pipelined(pl.program_id(0), pl.num_programs(0), pos_ref, pos_next_ref, y_hbm, buf_ref, sem,
                             TOP_K * tc)
    ns = x_ref.shape[1] // LANES
    rb = min(tc, COMBINE_ROWS)
    for r0 in range(0, tc, rb):
        wk = [jnp.broadcast_to(wt_ref[r0:r0 + rb, k:k + 1], (rb, LANES)) for k in range(TOP_K)]
        for s in range(ns):
            sl = slice(s * LANES, (s + 1) * LANES)
            out = wk[0] * buf_ref[slot, pl.ds(r0 * ns + s, rb, stride=ns), :]
            for k in range(1, TOP_K):
                out = out + wk[k] * buf_ref[slot, pl.ds((k * tc + r0) * ns + s, rb, stride=ns), :]
            o_ref[r0:r0 + rb, sl] = x_ref[r0:r0 + rb, sl] + gt_ref[:, sl] * out


def _combine(x2, gt, wts, pos, y_sorted, seq):
    t, d = x2.shape
    tc = _tile(seq, 128)
    per_b = seq // tc
    steps = t // tc
    ns = d // LANES
    y_sorted = y_sorted.reshape(-1, ns, LANES)
    pos_tiles = jnp.swapaxes(pos.reshape(steps, tc, TOP_K), 1, 2).reshape(steps, 1, TOP_K * tc)
    idx_spec = lambda f: pl.BlockSpec((None, 1, TOP_K * tc), f, memory_space=pltpu.SMEM)
    return pl.pallas_call(
        functools.partial(_combine_kernel, tc=tc),
        grid=(steps,),
        in_specs=[
            idx_spec(lambda i: (i, 0, 0)),
            idx_spec(lambda i: (jnp.minimum(i + 1, steps - 1), 0, 0)),
            pl.BlockSpec((tc, d), lambda i: (i, 0)),
            pl.BlockSpec((None, 1, d), lambda i: (i // per_b, 0, 0)),
            pl.BlockSpec((tc, LANES), lambda i: (i, 0)),
            pl.BlockSpec(memory_space=pl.ANY),
        ],
        out_specs=pl.BlockSpec((tc, d), lambda i: (i, 0)),
        out_shape=jax.ShapeDtypeStruct((t, d), F32),
        scratch_shapes=[pltpu.VMEM((2, TOP_K * tc * ns, LANES), F32), pltpu.SemaphoreType.DMA((2,))],
        compiler_params=_params("arbitrary"),
        name="moe_combine",
    )(pos_tiles, pos_tiles, x2, gt, wts, y_sorted)


def _routing_plan(top_idx, n_exp, tm):
    t = top_idx.shape[0]
    pairs = t * TOP_K
    e_flat = top_idx.reshape(pairs)
    onehot = (e_flat[:, None] == jnp.arange(n_exp, dtype=jnp.int32)[None, :]).astype(jnp.int32)
    csum = jnp.cumsum(onehot, axis=0)
    rank = jnp.sum(csum * onehot, axis=1) - 1
    counts = csum[-1]
    tiles_e = (counts + tm - 1) // tm
    tile_end = jnp.cumsum(tiles_e)
    tile_start = tile_end - tiles_e
    pos = tile_start[e_flat] * tm + rank
    n_tiles = pairs // tm + n_exp
    row_token = jnp.zeros((n_tiles * tm,), jnp.int32).at[pos].set(jnp.arange(pairs, dtype=jnp.int32) // TOP_K)
    tile_ids = jnp.arange(n_tiles, dtype=jnp.int32)
    owner = jnp.sum((tile_end[None, :] <= tile_ids[:, None]).astype(jnp.int32), axis=1)
    tile_expert = jnp.minimum(owner, n_exp - 1)
    tile_valid = (tile_ids < tile_end[-1]).astype(jnp.int32)
    last_valid = jnp.maximum(tile_end[-1] - 1, 0)
    tile_expert = jnp.where(tile_valid > 0, tile_expert, tile_expert[last_valid])
    return pos.astype(jnp.int32), row_token, tile_expert, tile_valid


def _rope_tables(positions, dim):
    inv_freq = 1.0 / (ROPE_THETA ** (jnp.arange(0, dim, 2, dtype=F32) / dim))
    ang = positions.astype(F32).reshape(-1, 1) * inv_freq
    return jnp.cos(ang), jnp.sin(ang)


def kernel(x, c, positions, w_mod, b_mod, g_norm1, g_norm2, w_in, g_kv, w_kv_up, w_proj_a, w_proj_b, w_out,
           w_router, b_router, w_moe1, b_moe1, w_moe2, b_moe2, g_final):
    bsz, seq, d = x.shape
    depth = w_mod.shape[0]
    n_exp = w_router.shape[-1]
    t = bsz * seq
    n_sel = min(TOPK_MAX, seq // 4)
    ck = _tile(seq, 512)
    tm_moe = 512 if (t * TOP_K) % 512 == 0 else 64

    cos, sin = _rope_tables(positions, HEAD_DIM)
    cos_a = jnp.concatenate([cos, cos], axis=1)
    sin_a = jnp.concatenate([-sin, sin], axis=1)
    cos_i, sin_i = _rope_tables(positions, IDX_DIM)
    cos_q = jnp.tile(cos_i, (1, LANES // (IDX_DIM // 2)))
    sin_q = jnp.tile(jnp.concatenate([-sin_i, sin_i], axis=1), (1, LANES // IDX_DIM))
    lane_is_k = (jnp.arange(LANES) < IDX_DIM)[None, :]
    cos_k = jnp.where(lane_is_k, cos_q, 1.0)
    sin_k = jnp.where(lane_is_k, sin_q, 0.0)

    mod = _modulation(c, w_mod, b_mod)
    x2 = x.reshape(t, d)

    o1, o2, o3, o4, o5, o6 = (A_Q_WIDTH, A_Q_WIDTH + KV_RANK, A_Q_WIDTH + KV_RANK + IDX_Q_WIDTH,
                              A_Q_WIDTH + KV_RANK + IDX_Q_WIDTH + IDX_DIM,
                              A_Q_WIDTH + KV_RANK + IDX_Q_WIDTH + IDX_DIM + IDX_HEADS,
                              A_Q_WIDTH + KV_RANK + IDX_Q_WIDTH + IDX_DIM + IDX_HEADS + B_QKV_WIDTH)

    for l in range(depth):
        mvec = lambda n: mod[l, :, n * d:(n + 1) * d].reshape(bsz, 1, d)
        sh1, sc1, gt1, sh2, sc2, gt2 = (mvec(n) for n in range(N_MOD))

        wl = w_in[l]
        w_qa = wl[:, :o1].astype(BF16)
        w_ckv = wl[:, o1:o2].astype(BF16)
        w_qi = wl[:, o2:o3].astype(BF16)
        w_kw = jnp.pad(wl[:, o3:o5], ((0, 0), (0, LANES - (o5 - o3)))).astype(BF16)
        w_qkvb = wl[:, o5:o6].reshape(d, 3, B_GROUPS, B_OUT_WIDTH)
        w_b = [w_qkvb[:, :, g].reshape(d, 3 * B_OUT_WIDTH).astype(BF16) for g in range(B_GROUPS)]
        w_gate = wl[:, o6:].astype(BF16)
        wkv = w_kv_up[l].reshape(KV_RANK, A_HEADS, 2, HEAD_DIM)
        w_k = wkv[:, :, 0].reshape(KV_RANK, A_Q_WIDTH).astype(BF16)
        w_v = wkv[:, :, 1].reshape(KV_RANK, A_Q_WIDTH).astype(BF16)

        h = _norm(x2, g_norm1[l], seq, sh1, sc1)
        qa = _mm(h, w_qa, epi="rope64", tables=(cos_a, sin_a))
        ka, va = _kv_proj(h, w_ckv, g_kv[l], w_k, w_v, cos_a, sin_a)
        qi = _mm(h, w_qi, epi="rope32", tables=(cos_q, sin_q))
        kw = _mm(h, w_kw, epi="rope32", tables=(cos_k, sin_k), out_dtype=F32)
        ki = kw[:, :IDX_DIM].astype(BF16).reshape(bsz, seq, IDX_DIM)
        w_t = jnp.swapaxes(kw[:, IDX_DIM:IDX_DIM + IDX_HEADS].reshape(bsz, seq, IDX_HEADS), 1, 2)
        qkv_b = [_mm(h, w_b[g], epi="rope64", tables=(cos_a, sin_a), rope_cols=2, tn_pref=B_OUT_WIDTH)
                 for g in range(B_GROUPS)]
        gates = _mm(h, w_gate, epi="sigmoid")

        bias = _index_select(qi.reshape(bsz, seq, IDX_Q_WIDTH), ki, w_t, n_sel, ck)
        o_a = _dsa_attention(qa.reshape(bsz, seq, A_Q_WIDTH), ka.reshape(bsz, seq, A_Q_WIDTH),
                             va.reshape(bsz, seq, A_Q_WIDTH), bias, ck)
        o_b, lse_b = zip(*[_dilated_group(qkv_b[g].reshape(bsz, seq, 3 * B_OUT_WIDTH), win, dil)
                           for g, (win, dil) in enumerate(B_PATTERNS)])
        merged = _merge(o_a.reshape(t, A_Q_WIDTH), o_b, lse_b, gates,
                        w_proj_a[l].astype(BF16), w_proj_b[l].astype(BF16))
        x2 = _mm(merged, w_out[l].astype(BF16), epi="resid", resid=(x2, gt1), seq=seq, out_dtype=F32,
                 tm_pref=512)

        h2 = _norm(x2, g_norm2[l], seq, sh2, sc2, out_dtype=F32, slab=True)
        w_r = jnp.pad(w_router[l], ((0, 0), (0, LANES - n_exp))).astype(BF16)
        b_r = jnp.pad(b_router[l], (0, LANES - n_exp)).reshape(1, LANES)
        top_idx, top_w = _router(h2, w_r, b_r, n_exp)
        pos, row_token, tile_expert, tile_valid = _routing_plan(top_idx[:, :TOP_K], n_exp, tm_moe)
        f2 = w_moe1.shape[-1]
        b1p = jnp.swapaxes(b_moe1[l].reshape(n_exp, f2 // (2 * LANES), LANES, 2), 2, 3).reshape(n_exp, 1, f2)
        y_sorted = _expert_ffn(h2, row_token, tile_expert, tile_valid, _deinterleave(w_moe1, l), b1p,
                               _cast_layer_bf16(w_moe2, l), b_moe2[l].reshape(n_exp, 1, d), tm_moe)
        x2 = _combine(x2, gt2, top_w, pos, y_sorted, seq)

    return _norm(x2, g_final, seq, out_dtype=x.dtype).reshape(bsz, seq, d)
```
